```python
import math
import jax, jax.numpy as jnp
from jax import lax
import numpy as np

D_MODEL = 1024
BATCH = 4
SEQ = 4096
DEPTH = 4
DEC_BATCH = 128
DEC_SEQ = 8
PAST_LEN = 2048
PAGE_SIZE = 128

D_MIX = D_MODEL
H_M = 5
HD_M = 128
D_M = H_M * HD_M
H_A = 6
HD_A = 64
D_A = H_A * HD_A
CONV_W = 4
MLSTM_CHUNK = 64
DILATED_PATTERNS = ((128, 1), (512, 4), (2048, 16))
MAX_WINDOW = 2048
D_FF = ((8 * D_MODEL + 767) // 768) * 256
N_IN = 4 * D_M + 2 * H_M + 3 * D_A
EPS = 1e-6

kernel_name = "hymba_mlstm_dilated_swa_decoder_step"


def rms_norm(x, g):
    xf = x.astype(jnp.float32)
    y = xf * lax.rsqrt(jnp.mean(xf * xf, axis=-1, keepdims=True) + EPS)
    return (y * g.astype(jnp.float32)).astype(x.dtype)


def alibi_slopes():
    return 2.0 ** (-8.0 * jnp.arange(1, H_A + 1, dtype=jnp.float32) / H_A)


def causal_conv(u, buf, w):
    T = u.shape[1]
    cat = jnp.concatenate([buf.astype(u.dtype), u], axis=1)
    out = sum(cat[:, i:i + T] * w[i] for i in range(CONV_W))
    return out, cat[:, T:]


def mlstm_chunkwise(q, k, v, ig, lf, C0, n0, m0):
    B, T, H, D = q.shape
    lc = math.gcd(T, MLSTM_CHUNK)
    nc = T // lc
    f32 = jnp.float32

    def chunks(a):
        a = a.astype(f32)
        return jnp.swapaxes(a.reshape((B, nc, lc) + a.shape[2:]), 0, 1)

    causal = jnp.tril(jnp.ones((lc, lc), dtype=bool))

    def step(carry, inp):
        C, n, m = carry
        qc, kc, vc, ic, fc = inp
        b = jnp.swapaxes(jnp.cumsum(fc, axis=1), 1, 2)
        ih = jnp.swapaxes(ic, 1, 2)
        dlog = jnp.where(causal, b[..., :, None] - b[..., None, :] + ih[..., None, :], -jnp.inf)
        inter = b + m[..., None]
        mt = jnp.maximum(inter, jnp.max(dlog, axis=-1))
        s = jnp.einsum('bthd,bshd->bhts', qc, kc) * jnp.exp(dlog - mt[..., None])
        e_inter = jnp.exp(inter - mt)
        num = jnp.einsum('bhts,bshd->bhtd', s, vc) + e_inter[..., None] * jnp.einsum('bhvk,bthk->bhtv', C, qc)
        den = jnp.sum(s, axis=-1) + e_inter * jnp.einsum('bhk,bthk->bht', n, qc)
        h = num / jnp.maximum(jnp.abs(den), jnp.exp(-mt))[..., None]
        b_last = b[..., -1]
        wlog = b_last[..., None] - b + ih
        m_new = jnp.maximum(b_last + m, jnp.max(wlog, axis=-1))
        w = jnp.exp(wlog - m_new[..., None])
        decay = jnp.exp(b_last + m - m_new)
        C_new = decay[..., None, None] * C + jnp.einsum('bhs,bshv,bshk->bhvk', w, vc, kc)
        n_new = decay[..., None] * n + jnp.einsum('bhs,bshk->bhk', w, kc)
        return (C_new, n_new, m_new), jnp.swapaxes(h, 1, 2)

    init = (C0.astype(f32), n0.astype(f32), m0.astype(f32))
    (C, n, m), hs = lax.scan(step, init, (chunks(q), chunks(k), chunks(v), chunks(ig), chunks(lf)))
    h = jnp.swapaxes(hs, 0, 1).reshape(B, T, H, D)
    return h, C, n, m


def combine_patterns(outs, lses):
    w = jax.nn.softmax(jnp.stack(lses, axis=0), axis=0)
    return jnp.sum(w[..., None] * jnp.stack(outs, axis=0), axis=0)


def dilated_attention_prompt(q, k, v, slopes):
    B, S, H, D = q.shape
    f32 = jnp.float32
    scale = HD_A ** -0.5
    outs, lses = [], []
    for win, dil in DILATED_PATTERNS:
        nw = win // dil
        L = S // dil
        nb = -(-L // nw)
        lp = nb * nw

        def to_blocks(a):
            a = jnp.transpose(a.reshape(B, L, dil, H, D), (0, 2, 1, 3, 4))
            a = jnp.pad(a, ((0, 0), (0, 0), (0, lp - L), (0, 0), (0, 0)))
            return a.reshape(B, dil, nb, nw, H, D)

        qb, kb, vb = to_blocks(q), to_blocks(k), to_blocks(v)
        pad_prev = ((0, 0), (0, 0), (1, 0), (0, 0), (0, 0), (0, 0))
        kc = jnp.concatenate([jnp.pad(kb, pad_prev)[:, :, :-1], kb], axis=3)
        vc = jnp.concatenate([jnp.pad(vb, pad_prev)[:, :, :-1], vb], axis=3)
        qi = jnp.arange(nw)[:, None]
        ki = jnp.arange(2 * nw)[None, :]
        dist = qi + nw - ki
        band = (dist >= 0) & (dist <= nw)
        valid = band[None] & ((jnp.arange(nb)[:, None, None] > 0) | (ki[None] >= nw))
        bias = -slopes[:, None, None] * (dist * dil).astype(f32)[None]
        s = jnp.einsum('brnqhd,brnkhd->brnhqk', qb, kc).astype(f32) * scale + bias
        s = jnp.where(valid[:, None], s, -jnp.inf)
        mx = jnp.max(s, axis=-1, keepdims=True)
        p = jnp.exp(s - mx)
        den = jnp.sum(p, axis=-1)
        o = jnp.einsum('brnhqk,brnkhd->brnqhd', p, vc.astype(f32)) / jnp.moveaxis(den, 3, 4)[..., None]
        lse = jnp.moveaxis(mx[..., 0] + jnp.log(den), 3, 4)
        o = jnp.transpose(o.reshape(B, dil, lp, H, D)[:, :, :L], (0, 2, 1, 3, 4)).reshape(B, S, H, D)
        lse = jnp.transpose(lse.reshape(B, dil, lp, H)[:, :, :L], (0, 2, 1, 3)).reshape(B, S, H)
        outs.append(o)
        lses.append(lse)
    return combine_patterns(outs, lses)


def dilated_attention_sample(q, k_all, v_all, slopes):
    B, T, H, D = q.shape
    f32 = jnp.float32
    scale = HD_A ** -0.5
    wb = k_all.shape[1] - T
    outs, lses = [], []
    for win, dil in DILATED_PATTERNS:
        nw = win // dil
        j = jnp.arange(nw + 1)
        idx = wb + jnp.arange(T)[:, None] - j[None, :] * dil
        valid = idx >= 0
        idx = jnp.maximum(idx, 0)
        kg = k_all[:, idx]
        vg = v_all[:, idx]
        bias = -slopes[:, None, None] * (j * dil).astype(f32)[None, None, :]
        s = jnp.einsum('bthd,btjhd->bhtj', q, kg).astype(f32) * scale + bias
        s = jnp.where(valid, s, -jnp.inf)
        mx = jnp.max(s, axis=-1, keepdims=True)
        p = jnp.exp(s - mx)
        den = jnp.sum(p, axis=-1)
        o = jnp.einsum('bhtj,btjhd->bthd', p, vg.astype(f32)) / jnp.swapaxes(den, 1, 2)[..., None]
        lse = jnp.swapaxes(mx[..., 0] + jnp.log(den), 1, 2)
        outs.append(o)
        lses.append(lse)
    return combine_patterns(outs, lses)


def hybrid_layer(x, params, conv_buf, C0, n0, m0, kv_past):
    (g_mix_pre, g_mix_post, g_ffn_pre, g_ffn_post, w_in, b_gates, w_conv, g_mh,
     w_out, w_gate, w_up, w_down) = params
    B, T, _ = x.shape
    f32 = jnp.float32
    a = rms_norm(x, g_mix_pre)
    z = a @ w_in
    cuts = [2 * D_M, 3 * D_M, 4 * D_M, 4 * D_M + 2 * H_M,
            4 * D_M + 2 * H_M + D_A, 4 * D_M + 2 * H_M + 2 * D_A]
    qk_pre, v_m, o_m, gates, q_a, k_a, v_a = jnp.split(z, cuts, axis=-1)

    qk, new_conv = causal_conv(qk_pre, conv_buf, w_conv)
    qk = jax.nn.silu(qk)
    q_m = qk[..., :D_M].reshape(B, T, H_M, HD_M)
    k_m = qk[..., D_M:].reshape(B, T, H_M, HD_M) * (HD_M ** -0.5)
    gates = gates.astype(f32) + b_gates.astype(f32)
    h_m, C, n, m = mlstm_chunkwise(q_m, k_m, v_m.reshape(B, T, H_M, HD_M),
                                   gates[..., :H_M], jax.nn.log_sigmoid(gates[..., H_M:]),
                                   C0, n0, m0)
    h_m = rms_norm(h_m.astype(x.dtype), g_mh.reshape(H_M, HD_M)).reshape(B, T, D_M)
    h_m = jax.nn.sigmoid(o_m) * h_m

    q_a = q_a.reshape(B, T, H_A, HD_A)
    k_a = k_a.reshape(B, T, H_A, HD_A)
    v_a = v_a.reshape(B, T, H_A, HD_A)
    slopes = alibi_slopes()
    if kv_past is None:
        h_a = dilated_attention_prompt(q_a, k_a, v_a, slopes)
        kv_new = jnp.stack([k_a, v_a], axis=2)[:, T - min(MAX_WINDOW, T):]
    else:
        k_all = jnp.concatenate([kv_past[:, :, 0].astype(k_a.dtype), k_a], axis=1)
        v_all = jnp.concatenate([kv_past[:, :, 1].astype(v_a.dtype), v_a], axis=1)
        h_a = dilated_attention_sample(q_a, k_all, v_all, slopes)
        kv_new = jnp.stack([k_a, v_a], axis=2)

    mix = jnp.concatenate([h_m, h_a.astype(x.dtype).reshape(B, T, D_A)], axis=-1) @ w_out
    x = x + rms_norm(mix, g_mix_post)
    hf = rms_norm(x, g_ffn_pre)
    f = (jax.nn.silu(hf @ w_gate) * (hf @ w_up)) @ w_down
    x = x + rms_norm(f, g_ffn_post)
    return x, kv_new, C, n, m, new_conv


def setup_inputs(seed: int = 0) -> dict:
    key = jax.random.key(seed)
    ks = jax.random.split(key, 24)
    f32 = jnp.float32
    wb = min(MAX_WINDOW, PAST_LEN)

    def nrm(k, shape, s=1.0):
        return s * jax.random.normal(k, shape, f32)

    return {
        "x_prompt": nrm(ks[0], (BATCH, SEQ, D_MODEL)),
        "x_sample": nrm(ks[1], (DEC_BATCH, DEC_SEQ, D_MODEL)),
        "cache_kv": nrm(ks[2], (DEPTH, DEC_BATCH, wb, 2, H_A, HD_A)),
        "state_C": nrm(ks[3], (DEPTH, DEC_BATCH, H_M, HD_M, HD_M), 0.05),
        "state_n": nrm(ks[4], (DEPTH, DEC_BATCH, H_M, HD_M), 0.1),
        "state_m": nrm(ks[5], (DEPTH, DEC_BATCH, H_M), 0.5),
        "state_conv": nrm(ks[6], (DEPTH, DEC_BATCH, CONV_W - 1, 2 * D_M)),
        "g_mix_pre": 1.0 + nrm(ks[7], (DEPTH, D_MODEL), 0.02),
        "g_mix_post": 1.0 + nrm(ks[8], (DEPTH, D_MODEL), 0.02),
        "g_ffn_pre": 1.0 + nrm(ks[9], (DEPTH, D_MODEL), 0.02),
        "g_ffn_post": 1.0 + nrm(ks[10], (DEPTH, D_MODEL), 0.02),
        "w_in": nrm(ks[11], (DEPTH, D_MODEL, N_IN), D_MODEL ** -0.5),
        "b_gates": jnp.concatenate(
            [nrm(ks[12], (DEPTH, H_M), 0.1),
             jnp.linspace(3.0, 6.0, H_M, dtype=f32)[None] + nrm(ks[13], (DEPTH, H_M), 0.1)], axis=-1),
        "w_conv": nrm(ks[14], (DEPTH, CONV_W, 2 * D_M), CONV_W ** -0.5),
        "g_mh": 1.0 + nrm(ks[15], (DEPTH, D_M), 0.02),
        "w_out": nrm(ks[16], (DEPTH, D_MIX, D_MODEL), D_MIX ** -0.5),
        "w_gate": nrm(ks[17], (DEPTH, D_MODEL, D_FF), D_MODEL ** -0.5),
        "w_up": nrm(ks[18], (DEPTH, D_MODEL, D_FF), D_MODEL ** -0.5),
        "w_down": nrm(ks[19], (DEPTH, D_FF, D_MODEL), D_FF ** -0.5),
    }


def reference(x_prompt, x_sample, cache_kv, state_C, state_n, state_m, state_conv,
              g_mix_pre, g_mix_post, g_ffn_pre, g_ffn_post, w_in, b_gates, w_conv, g_mh,
              w_out, w_gate, w_up, w_down):
    f32 = jnp.float32
    B = x_prompt.shape[0]
    hp, hs = x_prompt, x_sample
    kv_p, C_p, n_p, m_p, cv_p = [], [], [], [], []
    kv_s, C_s, n_s, m_s, cv_s = [], [], [], [], []
    for l in range(DEPTH):
        params = (g_mix_pre[l], g_mix_post[l], g_ffn_pre[l], g_ffn_post[l], w_in[l], b_gates[l],
                  w_conv[l], g_mh[l], w_out[l], w_gate[l], w_up[l], w_down[l])
        hp, kv, C, n, m, cv = hybrid_layer(
            hp, params,
            jnp.zeros((B, CONV_W - 1, 2 * D_M), hp.dtype),
            jnp.zeros((B, H_M, HD_M, HD_M), f32),
            jnp.zeros((B, H_M, HD_M), f32),
            jnp.zeros((B, H_M), f32),
            None)
        kv_p.append(kv); C_p.append(C); n_p.append(n); m_p.append(m); cv_p.append(cv)
        hs, kv, C, n, m, cv = hybrid_layer(
            hs, params, state_conv[l], state_C[l], state_n[l], state_m[l], cache_kv[l])
        kv_s.append(kv); C_s.append(C); n_s.append(n); m_s.append(m); cv_s.append(cv)
    return (hp, hs,
            jnp.stack(kv_p), jnp.stack(kv_s),
            jnp.stack(C_p), jnp.stack(n_p), jnp.stack(m_p), jnp.stack(cv_p),
            jnp.stack(C_s), jnp.stack(n_s), jnp.stack(m_s), jnp.stack(cv_s))
```

```python
import functools
import math

import numpy as np
import jax
import jax.numpy as jnp
from jax import lax
from jax.experimental import pallas as pl
from jax.experimental.pallas import tpu as pltpu

H_M = 5
HD_M = 128
D_M = H_M * HD_M
H_A = 6
HD_A = 64
D_A = H_A * HD_A
CONV_W = 4
MLSTM_CHUNK = 64
DILATED_PATTERNS = ((128, 1), (512, 4), (2048, 16))
MAX_WINDOW = 2048
EPS = 1e-6
LANES = 128
GATE_PAD = LANES
ATT_BLOCK = 128
VMEM_LIMIT = 56 * 1024 * 1024

_SLOPES = [2.0 ** (-8.0 * (h + 1) / H_A) for h in range(H_A)]
_NEG_INF = float("-inf")


def _bf(x):
    return x.astype(jnp.bfloat16)


def _dot(a, b):
    return jnp.dot(a, b, preferred_element_type=jnp.float32)


def _dot_nt(a, b):
    return lax.dot_general(a, b, (((1,), (1,)), ((), ())), preferred_element_type=jnp.float32)


def _rms(x, g):
    return x * lax.rsqrt(jnp.mean(x * x, axis=-1, keepdims=True) + EPS) * g


def _resident(shape):
    nd = len(shape)
    return pl.BlockSpec(shape, lambda *_: (0,) * nd, pipeline_mode=pl.Buffered(1))


_IN_CUTS = (0, 2 * D_M, 3 * D_M, 4 * D_M, 4 * D_M + D_A, 4 * D_M + 3 * D_A)


def _inproj_kernel(x_ref, g_ref, w_ref, wg_ref, qk_ref, vm_ref, om_ref, qa_ref, kva_ref, gates_ref):
    a = _bf(_rms(x_ref[...], g_ref[...]))
    outs = (qk_ref, vm_ref, om_ref, qa_ref, kva_ref)
    for o_ref, lo, hi in zip(outs, _IN_CUTS[:-1], _IN_CUTS[1:]):
        o_ref[...] = _dot(a, w_ref[:, lo:hi])
    gates_ref[...] = _dot(a, wg_ref[...])


def _inproj(x, g, w_main, w_gates, tm):
    n, d = x.shape
    widths = [hi - lo for lo, hi in zip(_IN_CUTS[:-1], _IN_CUTS[1:])] + [GATE_PAD]
    row = lambda w: pl.BlockSpec((tm, w), lambda i: (i, 0))
    return pl.pallas_call(
        _inproj_kernel,
        grid=(n // tm,),
        in_specs=[row(d), _resident((1, d)), _resident(w_main.shape), _resident(w_gates.shape)],
        out_specs=[row(w) for w in widths],
        out_shape=[jax.ShapeDtypeStruct((n, w), jnp.float32) for w in widths],
        compiler_params=pltpu.CompilerParams(dimension_semantics=("arbitrary",), vmem_limit_bytes=VMEM_LIMIT),
        name="inproj",
    )(x, g, w_main, w_gates)


_CONV_PAD = 8


def _mlstm_kernel(qk_ref, v_ref, o_ref, g_ref, bias_ref, wconv_ref, gmh_ref, conv0_ref, c0_ref, n0_ref, m0_ref,
                  h_ref, c_out, n_out, m_out, conv_out, ubuf, c_s, n_s, m_s, *, chunk):
    L = chunk
    c = pl.program_id(1)
    hist = CONV_W - 1

    @pl.when(c == 0)
    def _():
        ubuf[_CONV_PAD - hist:_CONV_PAD, :] = conv0_ref[0]
        c_s[...] = c0_ref[0, 0]
        n_s[...] = n0_ref[0]
        m_s[...] = m0_ref[0]

    ubuf[_CONV_PAD:_CONV_PAD + L, :] = qk_ref[0]
    conv = ubuf[_CONV_PAD - hist:_CONV_PAD - hist + L, :] * wconv_ref[0:1, :]
    for i in range(1, CONV_W):
        conv = conv + ubuf[_CONV_PAD - hist + i:_CONV_PAD - hist + i + L, :] * wconv_ref[i:i + 1, :]
    tail = ubuf[_CONV_PAD + L - hist:_CONV_PAD + L, :]
    ubuf[_CONV_PAD - hist:_CONV_PAD, :] = tail
    conv_out[0] = tail
    qk = conv * jax.nn.sigmoid(conv)

    gates = g_ref[0] + bias_ref[...]
    lf = jnp.minimum(gates, 0.0) - jnp.log(1.0 + jnp.exp(-jnp.abs(gates)))
    row_id = lax.broadcasted_iota(jnp.int32, (L, GATE_PAD), 0)
    bcum = lf
    step = 1
    while step < L:
        bcum = bcum + jnp.where(row_id >= step, pltpu.roll(bcum, step, 0), 0.0)
        step *= 2

    ti = lax.broadcasted_iota(jnp.int32, (L, L), 0)
    si = lax.broadcasted_iota(jnp.int32, (L, L), 1)
    causal = si <= ti
    eye = si == ti

    for h in range(H_M):
        q = qk[:, h * HD_M:(h + 1) * HD_M]
        k = qk[:, D_M + h * HD_M:D_M + (h + 1) * HD_M] * (HD_M ** -0.5)
        v = v_ref[0, :, h * HD_M:(h + 1) * HD_M]
        qb, kb, vb = _bf(q), _bf(k), _bf(v)
        ig = gates[:, h:h + 1]
        b = bcum[:, H_M + h:H_M + h + 1]
        m = m_s[h][:, 0:1]
        cmat = c_s[h]
        nvec = n_s[h]

        a_col = ig - b
        a_row = jnp.sum(jnp.where(eye, a_col, 0.0), axis=0, keepdims=True)
        dlog = jnp.where(causal, b + a_row, _NEG_INF)
        inter = b + m
        mt = jnp.maximum(inter, jnp.max(dlog, axis=1, keepdims=True))
        s = _dot_nt(qb, kb) * jnp.exp(dlog - mt)
        e_inter = jnp.exp(inter - mt)
        num = _dot(_bf(s), vb) + e_inter * _dot_nt(qb, _bf(cmat))
        den = jnp.sum(s, axis=1, keepdims=True) + e_inter * jnp.sum(q * nvec, axis=1, keepdims=True)
        hh = num / jnp.maximum(jnp.abs(den), jnp.exp(-mt))

        b_last = b[L - 1:L, :]
        wlog = b_last + a_col
        m_new = jnp.maximum(b_last + m, jnp.max(wlog, axis=0, keepdims=True))
        w = jnp.exp(wlog - m_new)
        decay = jnp.exp(b_last + m - m_new)
        wv = w * v
        if L < HD_M:
            pad = jnp.zeros((HD_M - L, HD_M), jnp.float32)
            wv_t = _bf(jnp.concatenate([wv, pad], axis=0).T)
            k_pad = _bf(jnp.concatenate([k, pad], axis=0))
        else:
            wv_t = _bf(wv.T)
            k_pad = kb
        c_s[h] = decay * cmat + _dot(wv_t, k_pad)
        n_s[h] = decay * nvec + jnp.sum(w * k, axis=0, keepdims=True)
        m_s[h] = jnp.broadcast_to(m_new, (1, LANES))

        hn = _rms(hh, gmh_ref[:, h * HD_M:(h + 1) * HD_M])
        h_ref[0, :, h * HD_M:(h + 1) * HD_M] = jax.nn.sigmoid(o_ref[0, :, h * HD_M:(h + 1) * HD_M]) * hn

    @pl.when(c == pl.num_programs(1) - 1)
    def _():
        c_out[0] = c_s[...]
        n_out[0] = n_s[...]
        m_out[0] = m_s[...]


def _mlstm(qk, vm, om, gates, b_gates, w_conv, g_mh, conv0, c0, n0, m0, chunk, layer):
    B, T, _ = qk.shape
    nc = T // chunk
    tok = lambda w: pl.BlockSpec((1, chunk, w), lambda b, c: (b, c, 0))
    per_b = lambda shape: pl.BlockSpec((1,) + shape, lambda b, c: (b,) + (0,) * len(shape))
    hist = CONV_W - 1
    f32 = jnp.float32
    return pl.pallas_call(
        functools.partial(_mlstm_kernel, chunk=chunk),
        grid=(B, nc),
        in_specs=[tok(2 * D_M), tok(D_M), tok(D_M), tok(GATE_PAD),
                  _resident((1, GATE_PAD)), _resident((CONV_W, 2 * D_M)), _resident((1, D_M)),
                  per_b((hist, 2 * D_M)),
                  pl.BlockSpec((1, 1, H_M, HD_M, HD_M), lambda b, c: (layer, b, 0, 0, 0)),
                  per_b((H_M, 1, HD_M)), per_b((H_M, 1, LANES))],
        out_specs=[tok(D_M), per_b((H_M, HD_M, HD_M)), per_b((H_M, 1, HD_M)), per_b((H_M, 1, LANES)),
                   per_b((hist, 2 * D_M))],
        out_shape=[jax.ShapeDtypeStruct((B, T, D_M), f32),
                   jax.ShapeDtypeStruct((B, H_M, HD_M, HD_M), f32),
                   jax.ShapeDtypeStruct((B, H_M, 1, HD_M), f32),
                   jax.ShapeDtypeStruct((B, H_M, 1, LANES), f32),
                   jax.ShapeDtypeStruct((B, hist, 2 * D_M), f32)],
        scratch_shapes=[pltpu.VMEM((_CONV_PAD + chunk, 2 * D_M), f32),
                        pltpu.VMEM((H_M, HD_M, HD_M), f32),
                        pltpu.VMEM((H_M, 1, HD_M), f32),
                        pltpu.VMEM((H_M, 1, LANES), f32)],
        compiler_params=pltpu.CompilerParams(dimension_semantics=("arbitrary", "arbitrary"),
                                             vmem_limit_bytes=VMEM_LIMIT),
        name="mlstm",
    )(qk, vm, om, gates, b_gates, w_conv, g_mh, conv0, c0, n0, m0)


def _pattn_kernel(*refs, dil, first, last):
    if first:
        q_ref, kvc_ref, kvp_ref = refs[:3]
        outs = refs[3:]
    else:
        q_ref, kvc_ref, kvp_ref, acc_in, m_in, l_in = refs[:6]
        outs = refs[6:]
    n = pl.program_id(2)
    nb = ATT_BLOCK
    scale = HD_A ** -0.5
    qi = lax.broadcasted_iota(jnp.int32, (nb, nb), 0)
    ki = lax.broadcasted_iota(jnp.int32, (nb, nb), 1)
    dist_c = qi - ki
    valid_c = dist_c >= 0
    dist_p = dist_c + nb
    valid_p = (dist_p <= nb) & (n > 0)
    fdist_c = (dist_c * dil).astype(jnp.float32)
    fdist_p = (dist_p * dil).astype(jnp.float32)

    acc_o, m_o, l_o = [], [], []
    for h in range(H_A):
        lo, hi = h * HD_A, (h + 1) * HD_A
        qh = _bf(q_ref[0, :, lo:hi])
        kc, vc = _bf(kvc_ref[0, :, lo:hi]), _bf(kvc_ref[0, :, D_A + lo:D_A + hi])
        kp, vp = _bf(kvp_ref[0, :, lo:hi]), _bf(kvp_ref[0, :, D_A + lo:D_A + hi])
        sc = jnp.where(valid_c, _dot_nt(qh, kc) * scale - _SLOPES[h] * fdist_c, _NEG_INF)
        sp = jnp.where(valid_p, _dot_nt(qh, kp) * scale - _SLOPES[h] * fdist_p, _NEG_INF)
        mx = jnp.maximum(jnp.max(sc, axis=1, keepdims=True), jnp.max(sp, axis=1, keepdims=True))
        if not first:
            m_prev = m_in[0, :, lo:lo + 1]
            mx = jnp.maximum(mx, m_prev)
        pc = jnp.exp(sc - mx)
        pp = jnp.exp(sp - mx)
        l = jnp.sum(pc, axis=1, keepdims=True) + jnp.sum(pp, axis=1, keepdims=True)
        acc = _dot(_bf(pc), vc) + _dot(_bf(pp), vp)
        if not first:
            alpha = jnp.exp(m_prev - mx)
            l = l + alpha * l_in[0, :, lo:lo + 1]
            acc = acc + alpha * acc_in[0, :, lo:hi]
        if last:
            acc_o.append(acc / l)
        else:
            acc_o.append(acc)
            m_o.append(jnp.broadcast_to(mx, (nb, HD_A)))
            l_o.append(jnp.broadcast_to(l, (nb, HD_A)))
    outs[0][0] = jnp.concatenate(acc_o, axis=1)
    if not last:
        outs[1][0] = jnp.concatenate(m_o, axis=1)
        outs[2][0] = jnp.concatenate(l_o, axis=1)


def _pattn(q, kv, state, dil, first, last):
    B, S, _ = q.shape
    lc = S // dil
    nblk = lc // ATT_BLOCK
    view = lambda a, w: a.reshape(B, lc, dil * w)
    qspec = pl.BlockSpec((1, ATT_BLOCK, D_A), lambda b, r, n: (b, n, r))
    kvc = pl.BlockSpec((1, ATT_BLOCK, 2 * D_A), lambda b, r, n: (b, n, r))
    kvp = pl.BlockSpec((1, ATT_BLOCK, 2 * D_A), lambda b, r, n: (b, jnp.maximum(n - 1, 0), r))
    ins = [view(q, D_A), view(kv, 2 * D_A), view(kv, 2 * D_A)]
    in_specs = [qspec, kvc, kvp]
    if not first:
        ins += [view(s, D_A) for s in state]
        in_specs += [qspec] * 3
    n_out = 1 if last else 3
    res = pl.pallas_call(
        functools.partial(_pattn_kernel, dil=dil, first=first, last=last),
        grid=(B, dil, nblk),
        in_specs=in_specs,
        out_specs=[qspec] * n_out,
        out_shape=[jax.ShapeDtypeStruct((B, lc, dil * D_A), jnp.float32)] * n_out,
        compiler_params=pltpu.CompilerParams(dimension_semantics=("arbitrary",) * 3, vmem_limit_bytes=VMEM_LIMIT),
        name=f"pattn_d{dil}",
    )(*ins)
    return [r.reshape(B, S, D_A) for r in res]


def _prompt_attention(q, kv):
    state = None
    npat = len(DILATED_PATTERNS)
    for i, (win, dil) in enumerate(DILATED_PATTERNS):
        assert win // dil == ATT_BLOCK and (q.shape[1] // dil) % ATT_BLOCK == 0
        state = _pattn(q, kv, state, dil, first=(i == 0), last=(i == npat - 1))
    return state[0]


def _sample_tables(wb, t_new):
    rows = np.arange(H_A * t_new)
    t = rows % t_new
    slope = np.asarray(_SLOPES)[rows // t_new]
    pos = np.arange(wb + t_new)
    d = wb + t[:, None] - pos[None, :]
    cnt = np.zeros(d.shape, np.float32)
    for win, dil in DILATED_PATTERNS:
        cnt += ((d >= 0) & (d % dil == 0) & (d <= win)).astype(np.float32)
    bias = np.where(cnt > 0, -slope[:, None] * d, -np.inf).astype(np.float32)
    return bias, cnt


def _sattn_kernel(q_ref, kvn_ref, cache_ref, bias_ref, cnt_ref, biasn_ref, cntn_ref, o_ref, *, t_new):
    scale = HD_A ** -0.5
    heads = [(h * HD_A, (h + 1) * HD_A) for h in range(H_A)]
    qh = [_bf(q_ref[0, :, lo:hi]) for lo, hi in heads]
    sc = jnp.concatenate([_dot(qh[h], _bf(cache_ref[0, 0, 0, h])) for h in range(H_A)], axis=0)
    sn = jnp.concatenate([_dot_nt(qh[h], _bf(kvn_ref[0, :, lo:hi])) for h, (lo, hi) in enumerate(heads)], axis=0)
    sc = sc * scale + bias_ref[...]
    sn = sn * scale + biasn_ref[...]
    mx = jnp.maximum(jnp.max(sc, axis=1, keepdims=True), jnp.max(sn, axis=1, keepdims=True))
    pc = cnt_ref[...] * jnp.exp(sc - mx)
    pn = cntn_ref[...] * jnp.exp(sn - mx)
    den = jnp.sum(pc, axis=1, keepdims=True) + jnp.sum(pn, axis=1, keepdims=True)
    outs = []
    for h, (lo, hi) in enumerate(heads):
        r0, r1 = h * t_new, (h + 1) * t_new
        o = (_dot_nt(_bf(pc[r0:r1]), _bf(cache_ref[0, 0, 1, h]))
             + _dot(_bf(pn[r0:r1]), _bf(kvn_ref[0, :, D_A + lo:D_A + hi])))
        outs.append(o / den[r0:r1])
    o_ref[0] = jnp.concatenate(outs, axis=1)


def _sample_attention(q, kvn, cache_t, layer):
    B, T, _ = q.shape
    wb = cache_t.shape[-1]
    bias, cnt = _sample_tables(wb, T)
    per_b = lambda w: pl.BlockSpec((1, T, w), lambda b: (b, 0, 0))
    consts = [bias[:, :wb], cnt[:, :wb], bias[:, wb:], cnt[:, wb:]]
    return pl.pallas_call(
        functools.partial(_sattn_kernel, t_new=T),
        grid=(B,),
        in_specs=[per_b(D_A), per_b(2 * D_A),
                  pl.BlockSpec((1, 1, 2, H_A, HD_A, wb), lambda b: (layer, b, 0, 0, 0, 0))]
                 + [_resident(c.shape) for c in consts],
        out_specs=per_b(D_A),
        out_shape=jax.ShapeDtypeStruct((B, T, D_A), jnp.float32),
        compiler_params=pltpu.CompilerParams(dimension_semantics=("arbitrary",), vmem_limit_bytes=VMEM_LIMIT),
        name="sattn",
    )(q, kvn, cache_t, *[jnp.asarray(c) for c in consts])


def _outffn_kernel(x_ref, hm_ref, ha_ref, wo_ref, g1_ref, g2_ref, g3_ref, wg_ref, wu_ref, wd_ref, y_ref):
    mix = _dot(_bf(hm_ref[...]), wo_ref[0:D_M, :]) + _dot(_bf(ha_ref[...]), wo_ref[D_M:D_M + D_A, :])
    x1 = x_ref[...] + _rms(mix, g1_ref[...])
    hf = _bf(_rms(x1, g2_ref[...]))
    gate = _dot(hf, wg_ref[...])
    up = _dot(hf, wu_ref[...])
    f = _dot(_bf(gate * jax.nn.sigmoid(gate) * up), wd_ref[...])
    y_ref[...] = x1 + _rms(f, g3_ref[...])


def _outffn(x, hm, ha, w_out, g_mix_post, g_ffn_pre, g_ffn_post, w_gate, w_up, w_down, tm):
    n, d = x.shape
    row = lambda w: pl.BlockSpec((tm, w), lambda i: (i, 0))
    return pl.pallas_call(
        _outffn_kernel,
        grid=(n // tm,),
        in_specs=[row(d), row(D_M), row(D_A), _resident(w_out.shape), _resident((1, d)), _resident((1, d)),
                  _resident((1, d)), _resident(w_gate.shape), _resident(w_up.shape), _resident(w_down.shape)],
        out_specs=row(d),
        out_shape=jax.ShapeDtypeStruct((n, d), jnp.float32),
        compiler_params=pltpu.CompilerParams(dimension_semantics=("arbitrary",), vmem_limit_bytes=VMEM_LIMIT),
        name="outffn",
    )(x, hm, ha, w_out, g_mix_post, g_ffn_pre, g_ffn_post, w_gate, w_up, w_down)


def _row_tile(n, cap):
    t = min(n, cap)
    while n % t:
        t //= 2
    return t


def _group_layer(x, lw, conv0, c0, n0, m0, cache, layer):
    B, T, D = x.shape
    n = B * T
    xf = x.reshape(n, D)
    qk, vm, om, qa, kva, gates = _inproj(xf, lw["g_mix_pre"], lw["w_main"], lw["w_gates"], _row_tile(n, 512))
    r3 = lambda a: a.reshape(B, T, a.shape[-1])
    chunk = math.gcd(T, MLSTM_CHUNK)
    hm, c_new, n_new, m_new, conv_new = _mlstm(r3(qk), r3(vm), r3(om), r3(gates), lw["b_gates"], lw["w_conv"],
                                               lw["g_mh"], conv0, c0, n0, m0, chunk, layer if cache is not None else 0)
    if cache is None:
        ha = _prompt_attention(r3(qa), r3(kva))
        kv_new = r3(kva)[:, T - min(MAX_WINDOW, T):]
    else:
        ha = _sample_attention(r3(qa), r3(kva), cache, layer)
        kv_new = r3(kva)
    y = _outffn(xf, hm.reshape(n, D_M), ha.reshape(n, D_A), lw["w_out"], lw["g_mix_post"], lw["g_ffn_pre"],
                lw["g_ffn_post"], lw["w_gate"], lw["w_up"], lw["w_down"], _row_tile(n, 256))
    kv_new = kv_new.reshape(B, kv_new.shape[1], 2, H_A, HD_A)
    return (y.reshape(B, T, D), kv_new, c_new, n_new.reshape(B, H_M, HD_M), m_new[:, :, 0, 0], conv_new)


def kernel(x_prompt, x_sample, cache_kv, state_C, state_n, state_m, state_conv, g_mix_pre, g_mix_post, g_ffn_pre,
           g_ffn_post, w_in, b_gates, w_conv, g_mh, w_out, w_gate, w_up, w_down):
    f32 = jnp.float32
    depth = w_in.shape[0]
    B = x_prompt.shape[0]
    BS = x_sample.shape[0]
    d_model = x_prompt.shape[-1]
    hist = CONV_W - 1
    g0 = 4 * D_M
    g1 = g0 + 2 * H_M
    cache = jnp.transpose(cache_kv, (0, 1, 3, 4, 5, 2))

    zero_states = (jnp.zeros((B, hist, 2 * D_M), f32), jnp.zeros((1, B, H_M, HD_M, HD_M), f32),
                   jnp.zeros((B, H_M, 1, HD_M), f32), jnp.zeros((B, H_M, 1, LANES), f32))
    hp, hs = x_prompt, x_sample
    outs_p, outs_s = [], []
    for l in range(depth):
        lw = {
            "g_mix_pre": g_mix_pre[l][None], "g_mix_post": g_mix_post[l][None],
            "g_ffn_pre": g_ffn_pre[l][None], "g_ffn_post": g_ffn_post[l][None],
            "w_main": _bf(jnp.concatenate([w_in[l][:, :g0], w_in[l][:, g1:]], axis=1)),
            "w_gates": _bf(jnp.pad(w_in[l][:, g0:g1], ((0, 0), (0, GATE_PAD - 2 * H_M)))),
            "b_gates": jnp.pad(b_gates[l], (0, GATE_PAD - 2 * H_M))[None],
            "w_conv": w_conv[l], "g_mh": g_mh[l][None],
            "w_out": _bf(w_out[l]), "w_gate": _bf(w_gate[l]), "w_up": _bf(w_up[l]), "w_down": _bf(w_down[l]),
        }
        res = _group_layer(hp, lw, *zero_states, None, l)
        hp = res[0]
        outs_p.append(res[1:])
        m0 = jnp.broadcast_to(state_m[l][:, :, None, None], (BS, H_M, 1, LANES))
        res = _group_layer(hs, lw, state_conv[l], state_C, state_n[l][:, :, None, :], m0, cache, l)
        hs = res[0]
        outs_s.append(res[1:])
    stack = lambda outs, i: jnp.stack([o[i] for o in outs])
    return (hp, hs, stack(outs_p, 0), stack(outs_s, 0),
            stack(outs_p, 1), stack(outs_p, 2), stack(outs_p, 3), stack(outs_p, 4),
            stack(outs_s, 1), stack(outs_s, 2), stack(outs_s, 3), stack(outs_s, 4))
```

```python
import functools

import numpy as np
import jax
import jax.numpy as jnp
from jax import lax
from jax.experimental import pallas as pl
from jax.experimental.pallas import tpu as pltpu

H_M = 5
HD_M = 128
D_M = H_M * HD_M
H_A = 6
HD_A = 64
D_A = H_A * HD_A
CONV_W = 4
MLSTM_MAX_CHUNK = 256
MLSTM_GROUP_TOKENS = 256
MLSTM_MAX_GROUP = 4
DILATED_PATTERNS = ((128, 1), (512, 4), (2048, 16))
MAX_WINDOW = 2048
EPS = 1e-6
LANES = 128
GATE_PAD = LANES
ATT_BLOCK = 128
VMEM_LIMIT = 56 * 1024 * 1024

_SLOPES = [2.0 ** (-8.0 * (h + 1) / H_A) for h in range(H_A)]
_NEG_INF = float("-inf")


def _bf(x):
    return x.astype(jnp.bfloat16)


def _dot(a, b):
    return jnp.dot(a, b, preferred_element_type=jnp.float32)


def _dot_nt(a, b):
    return lax.dot_general(a, b, (((1,), (1,)), ((), ())), preferred_element_type=jnp.float32)


def _rms(x, g):
    return x * lax.rsqrt(jnp.mean(x * x, axis=-1, keepdims=True) + EPS) * g


def _resident(shape):
    nd = len(shape)
    return pl.BlockSpec(shape, lambda *_: (0,) * nd, pipeline_mode=pl.Buffered(1))


def _row_tile(n, cap):
    t = min(n, cap)
    while n % t:
        t //= 2
    return t


_IN_CUTS = (0, 2 * D_M, 3 * D_M, 4 * D_M, 4 * D_M + D_A, 4 * D_M + 3 * D_A)


def _inproj_kernel(x_ref, g_ref, w_ref, wg_ref, qk_ref, vm_ref, om_ref, qa_ref, kva_ref, gates_ref):
    a = _bf(_rms(x_ref[...], g_ref[...]))
    outs = (qk_ref, vm_ref, om_ref)
    for o_ref, lo, hi in zip(outs, _IN_CUTS[:3], _IN_CUTS[1:4]):
        o_ref[...] = _dot(a, w_ref[:, lo:hi])
    for o_ref, lo, hi in zip((qa_ref, kva_ref), _IN_CUTS[3:5], _IN_CUTS[4:6]):
        z = _dot(a, w_ref[:, lo:hi])
        for j in range((hi - lo) // LANES):
            o_ref[j] = z[:, j * LANES:(j + 1) * LANES]
    gates_ref[...] = _dot(a, wg_ref[...])


def _inproj(x, g, w_main, w_gates, tm):
    n, d = x.shape
    row = lambda w: pl.BlockSpec((tm, w), lambda i: (i, 0))
    tiles = lambda w: pl.BlockSpec((w // LANES, tm, LANES), lambda i: (0, i, 0))
    f32 = jnp.float32
    return pl.pallas_call(
        _inproj_kernel,
        grid=(n // tm,),
        in_specs=[row(d), _resident((1, d)), _resident(w_main.shape), _resident(w_gates.shape)],
        out_specs=[row(2 * D_M), row(D_M), row(D_M), tiles(D_A), tiles(2 * D_A), row(GATE_PAD)],
        out_shape=[jax.ShapeDtypeStruct((n, 2 * D_M), f32), jax.ShapeDtypeStruct((n, D_M), f32),
                   jax.ShapeDtypeStruct((n, D_M), f32), jax.ShapeDtypeStruct((D_A // LANES, n, LANES), f32),
                   jax.ShapeDtypeStruct((2 * D_A // LANES, n, LANES), f32),
                   jax.ShapeDtypeStruct((n, GATE_PAD), f32)],
        compiler_params=pltpu.CompilerParams(dimension_semantics=("arbitrary",), vmem_limit_bytes=VMEM_LIMIT),
        name="inproj",
    )(x, g, w_main, w_gates)


_CONV_PAD = 8


def _mlstm_head(q, k, v, a_col, b, a_row, m, cmat, nvec, causal, eye):
    L = q.shape[0]
    qb, kb, vb = _bf(q), _bf(k), _bf(v)
    if a_row is None:
        a_row = jnp.sum(jnp.where(eye, a_col, 0.0), axis=0, keepdims=True)
    dlog = jnp.where(causal, b + a_row, _NEG_INF)
    inter = b + m
    mt = jnp.maximum(inter, jnp.max(dlog, axis=1, keepdims=True))
    s = _dot_nt(qb, kb) * jnp.exp(dlog - mt)
    e_inter = jnp.exp(inter - mt)
    num = _dot(_bf(s), vb) + e_inter * _dot_nt(qb, _bf(cmat))
    den = jnp.sum(s, axis=1, keepdims=True) + e_inter * jnp.sum(q * nvec, axis=1, keepdims=True)
    hh = num / jnp.maximum(jnp.abs(den), jnp.exp(-mt))

    b_last = b[L - 1:L, :]
    wlog = b_last + a_col
    m_new = jnp.maximum(b_last + m, jnp.max(wlog, axis=0, keepdims=True))
    w = jnp.exp(wlog - m_new)
    decay = jnp.exp(b_last + m - m_new)
    wv = w * v
    if L < HD_M:
        pad = jnp.zeros((HD_M - L, HD_M), jnp.float32)
        wv_t = _bf(jnp.concatenate([wv, pad], axis=0).T)
        k_pad = _bf(jnp.concatenate([k, pad], axis=0))
    else:
        wv_t = _bf(wv.T)
        k_pad = kb
    c_new = decay * cmat + _dot(wv_t, k_pad)
    n_new = decay * nvec + jnp.sum(w * k, axis=0, keepdims=True)
    return hh, c_new, n_new, m_new


def _mlstm_kernel(qk_ref, v_ref, o_ref, g_ref, bias_ref, wconv_ref, gmh_ref, conv0_ref, c0_ref, n0_ref, m0_ref,
                  h_ref, c_out, n_out, m_out, conv_out, ubuf, c_s, n_s, m_s, *, chunk, group):
    L = chunk
    c = pl.program_id(1)
    hist = CONV_W - 1

    @pl.when(c == 0)
    def _():
        ubuf[:, _CONV_PAD - hist:_CONV_PAD, :] = conv0_ref[...]
        c_s[...] = c0_ref[0]
        n_s[...] = n0_ref[...]
        m_s[...] = m0_ref[...]

    row_id = lax.broadcasted_iota(jnp.int32, (L, GATE_PAD), 0)
    ti = lax.broadcasted_iota(jnp.int32, (L, L), 0)
    si = lax.broadcasted_iota(jnp.int32, (L, L), 1)
    causal = si <= ti
    eye = si == ti

    for g in range(group):
        ubuf[g, _CONV_PAD:_CONV_PAD + L, :] = qk_ref[g]
        conv = ubuf[g, _CONV_PAD - hist:_CONV_PAD - hist + L, :] * wconv_ref[0:1, :]
        for i in range(1, CONV_W):
            conv = conv + ubuf[g, _CONV_PAD - hist + i:_CONV_PAD - hist + i + L, :] * wconv_ref[i:i + 1, :]
        tail = ubuf[g, _CONV_PAD + L - hist:_CONV_PAD + L, :]
        ubuf[g, _CONV_PAD - hist:_CONV_PAD, :] = tail
        conv_out[g] = tail
        qk = conv * jax.nn.sigmoid(conv)

        gates = g_ref[g] + bias_ref[...]
        lf = jnp.minimum(gates, 0.0) - jnp.log(1.0 + jnp.exp(-jnp.abs(gates)))
        bcum = lf
        step = 1
        while step < L:
            bcum = bcum + jnp.where(row_id >= step, pltpu.roll(bcum, step, 0), 0.0)
            step *= 2

        a_all = gates - pltpu.roll(bcum, GATE_PAD - H_M, 1)
        a_rows = a_all.T if L % LANES == 0 else None

        for h in range(H_M):
            lanes = slice(h * HD_M, (h + 1) * HD_M)
            hh, c_new, n_new, m_new = _mlstm_head(
                qk[:, lanes], qk[:, D_M + h * HD_M:D_M + (h + 1) * HD_M] * (HD_M ** -0.5), v_ref[g, :, lanes],
                a_all[:, h:h + 1], bcum[:, H_M + h:H_M + h + 1], None if a_rows is None else a_rows[h:h + 1, :],
                m_s[g, h][:, 0:1], c_s[g, h], n_s[g, h], causal, eye)
            c_s[g, h] = c_new
            n_s[g, h] = n_new
            m_s[g, h] = jnp.broadcast_to(m_new, (1, LANES))
            h_ref[g, :, lanes] = jax.nn.sigmoid(o_ref[g, :, lanes]) * _rms(hh, gmh_ref[:, lanes])

    @pl.when(c == pl.num_programs(1) - 1)
    def _():
        c_out[...] = c_s[...]
        n_out[...] = n_s[...]
        m_out[...] = m_s[...]


def _mlstm_tiling(B, T):
    chunk = _row_tile(T, MLSTM_MAX_CHUNK)
    group = _row_tile(B, max(1, min(MLSTM_MAX_GROUP, MLSTM_GROUP_TOKENS // chunk)))
    return chunk, group


def _mlstm(qk, vm, om, gates, b_gates, w_conv, g_mh, conv0, c0, n0, m0, layer):
    B, T, _ = qk.shape
    chunk, G = _mlstm_tiling(B, T)
    tok = lambda w: pl.BlockSpec((G, chunk, w), lambda b, c: (b, c, 0))
    per_b = lambda shape: pl.BlockSpec((G,) + shape, lambda b, c: (b,) + (0,) * len(shape))
    hist = CONV_W - 1
    f32 = jnp.float32
    return pl.pallas_call(
        functools.partial(_mlstm_kernel, chunk=chunk, group=G),
        grid=(B // G, T // chunk),
        in_specs=[tok(2 * D_M), tok(D_M), tok(D_M), tok(GATE_PAD),
                  _resident((1, GATE_PAD)), _resident((CONV_W, 2 * D_M)), _resident((1, D_M)),
                  per_b((hist, 2 * D_M)),
                  pl.BlockSpec((1, G, H_M, HD_M, HD_M), lambda b, c: (layer, b, 0, 0, 0)),
                  per_b((H_M, 1, HD_M)), per_b((H_M, 1, LANES))],
        out_specs=[tok(D_M), per_b((H_M, HD_M, HD_M)), per_b((H_M, 1, HD_M)), per_b((H_M, 1, LANES)),
                   per_b((hist, 2 * D_M))],
        out_shape=[jax.ShapeDtypeStruct((B, T, D_M), f32),
                   jax.ShapeDtypeStruct((B, H_M, HD_M, HD_M), f32),
                   jax.ShapeDtypeStruct((B, H_M, 1, HD_M), f32),
                   jax.ShapeDtypeStruct((B, H_M, 1, LANES), f32),
                   jax.ShapeDtypeStruct((B, hist, 2 * D_M), f32)],
        scratch_shapes=[pltpu.VMEM((G, _CONV_PAD + chunk, 2 * D_M), f32),
                        pltpu.VMEM((G, H_M, HD_M, HD_M), f32),
                        pltpu.VMEM((G, H_M, 1, HD_M), f32),
                        pltpu.VMEM((G, H_M, 1, LANES), f32)],
        compiler_params=pltpu.CompilerParams(dimension_semantics=("arbitrary", "arbitrary"),
                                             vmem_limit_bytes=VMEM_LIMIT),
        name="mlstm",
    )(qk, vm, om, gates, b_gates, w_conv, g_mh, conv0, c0, n0, m0)


ATT_TILE = ATT_BLOCK * max(d for _, d in DILATED_PATTERNS)


def _pattn_kernel(q_ref, kv_ref, o_ref, prev_s, acc_s, m_s, l_s, s_buf, p_buf, m_buf):
    i = pl.program_id(1)
    nb = ATT_BLOCK
    pair = 2 * HD_A
    scale = HD_A ** -0.5

    @pl.when(i == 0)
    def _():
        prev_s[...] = jnp.zeros(prev_s.shape, prev_s.dtype)

    qi = lax.broadcasted_iota(jnp.int32, (nb, nb), 0)
    ki = lax.broadcasted_iota(jnp.int32, (nb, nb), 1)
    dist = qi - ki
    lo_half = lax.broadcasted_iota(jnp.int32, (nb, pair), 1) < HD_A
    ones = jnp.ones((2 * nb, pair), jnp.bfloat16)
    no_keys = jnp.full((nb, nb), _NEG_INF, jnp.float32)
    npat = len(DILATED_PATTERNS)
    npair = H_A // 2

    def rows_of(start, dil):
        return pl.ds(start, nb, stride=dil) if dil > 1 else pl.ds(start, nb)

    for pi, (win, dil) in enumerate(DILATED_PATTERNS):
        first, last = pi == 0, pi == npat - 1
        span = dil * nb
        nblk = ATT_TILE // span
        nd_cur = jnp.where(dist >= 0, -(dist * dil).astype(jnp.float32), _NEG_INF)
        nd_prev = jnp.where(dist <= 0, -((dist + nb) * dil).astype(jnp.float32), _NEG_INF)
        nd_prev_first = jnp.where(i > 0, nd_prev, no_keys)

        def unit(start, prev_ref, prev_start, nd_prev_u, dil=dil, first=first, last=last, nd_cur=nd_cur):
            rows = rows_of(start, dil)
            prows = rows_of(prev_start, dil)
            nd = jnp.concatenate([nd_cur, nd_prev_u], axis=1)
            for hp in range(npair):
                q2 = q_ref[hp, rows, :] * scale
                k2 = _bf(jnp.concatenate([kv_ref[hp, rows, :], prev_ref[hp, prows, :]], axis=0))
                for e in range(2):
                    qm = _bf(jnp.where(lo_half if e == 0 else jnp.logical_not(lo_half), q2, 0.0))
                    s_buf[2 * hp + e] = _dot_nt(qm, k2) + _SLOPES[2 * hp + e] * nd
            for h in range(H_A):
                s = s_buf[h]
                m = jnp.max(s, axis=1, keepdims=True)
                p_buf[h] = _bf(jnp.exp(s - m))
                m_buf[h] = jnp.broadcast_to(m, (nb, pair))
            for hp in range(npair):
                v2 = _bf(jnp.concatenate([kv_ref[npair + hp, rows, :], prev_ref[npair + hp, prows, :]], axis=0))
                rhs = jnp.concatenate([v2, ones], axis=1)
                r_e = _dot(p_buf[2 * hp], rhs)
                r_o = _dot(p_buf[2 * hp + 1], rhs)
                acc2 = jnp.where(lo_half, r_e[:, :pair], r_o[:, :pair])
                l2 = jnp.where(lo_half, r_e[:, pair:], r_o[:, pair:])
                m2 = jnp.where(lo_half, m_buf[2 * hp], m_buf[2 * hp + 1])
                if not first:
                    m_old = m_s[hp, rows, :]
                    m_new = jnp.maximum(m_old, m2)
                    a_old = jnp.exp(m_old - m_new)
                    a_loc = jnp.exp(m2 - m_new)
                    acc2 = a_old * acc_s[hp, rows, :] + a_loc * acc2
                    l2 = a_old * l_s[hp, rows, :] + a_loc * l2
                    m2 = m_new
                if last:
                    o_ref[hp, rows, :] = acc2 / l2
                else:
                    acc_s[hp, rows, :] = acc2
                    m_s[hp, rows, :] = m2
                    l_s[hp, rows, :] = l2

        def first_block(c, carry, nblk=nblk, span=span, unit=unit, nd_prev_first=nd_prev_first):
            unit(c, prev_s, c + (nblk - 1) * span, nd_prev_first)
            return carry

        def later_block(u, carry, dil=dil, span=span, unit=unit, nd_prev=nd_prev):
            start = u % dil + (1 + u // dil) * span
            unit(start, kv_ref, start - span, nd_prev)
            return carry

        lax.fori_loop(0, dil, first_block, 0)
        if nblk > 1:
            lax.fori_loop(0, dil * (nblk - 1), later_block, 0)

    prev_s[...] = kv_ref[...]


def _prompt_attention(q, kv, B):
    nq, n, _ = q.shape
    S = n // B
    for win, dil in DILATED_PATTERNS:
        assert win // dil == ATT_BLOCK and ATT_TILE % (dil * ATT_BLOCK) == 0
    assert S % ATT_TILE == 0 and LANES == 2 * HD_A
    nt = S // ATT_TILE
    tile = lambda k: pl.BlockSpec((k, ATT_TILE, LANES), lambda b, i: (0, b * nt + i, 0))
    f32 = jnp.float32
    return pl.pallas_call(
        _pattn_kernel,
        grid=(B, nt),
        in_specs=[tile(nq), tile(2 * nq)],
        out_specs=tile(nq),
        out_shape=jax.ShapeDtypeStruct((nq, n, LANES), f32),
        scratch_shapes=[pltpu.VMEM((2 * nq, ATT_TILE, LANES), f32), pltpu.VMEM((nq, ATT_TILE, LANES), f32),
                        pltpu.VMEM((nq, ATT_TILE, LANES), f32), pltpu.VMEM((nq, ATT_TILE, LANES), f32),
                        pltpu.VMEM((H_A, ATT_BLOCK, 2 * ATT_BLOCK), f32),
                        pltpu.VMEM((H_A, ATT_BLOCK, 2 * ATT_BLOCK), jnp.bfloat16),
                        pltpu.VMEM((H_A, ATT_BLOCK, LANES), f32)],
        compiler_params=pltpu.CompilerParams(dimension_semantics=("arbitrary", "arbitrary"),
                                             vmem_limit_bytes=VMEM_LIMIT),
        name="pattn",
    )(q, kv)


def _sample_tables(wb, t_new):
    rows = np.arange(H_A * t_new)
    t = rows % t_new
    slope = np.asarray(_SLOPES)[rows // t_new]
    pos = np.arange(wb + t_new)
    d = wb + t[:, None] - pos[None, :]
    cnt = np.zeros(d.shape, np.float32)
    for win, dil in DILATED_PATTERNS:
        cnt += ((d >= 0) & (d % dil == 0) & (d <= win)).astype(np.float32)
    bias = np.where(cnt > 0, -slope[:, None] * d, -np.inf).astype(np.float32)
    return bias, cnt


def _sattn_kernel(q_ref, kvn_ref, cache_ref, bias_ref, cnt_ref, biasn_ref, cntn_ref, o_ref, *, t_new):
    scale = HD_A ** -0.5
    heads = [(h * HD_A, (h + 1) * HD_A) for h in range(H_A)]
    qh = [_bf(q_ref[0, :, lo:hi]) for lo, hi in heads]
    sc = jnp.concatenate([_dot(qh[h], _bf(cache_ref[0, 0, 0, h])) for h in range(H_A)], axis=0)
    sn = jnp.concatenate([_dot_nt(qh[h], _bf(kvn_ref[0, :, lo:hi])) for h, (lo, hi) in enumerate(heads)], axis=0)
    sc = sc * scale + bias_ref[...]
    sn = sn * scale + biasn_ref[...]
    mx = jnp.maximum(jnp.max(sc, axis=1, keepdims=True), jnp.max(sn, axis=1, keepdims=True))
    pc = cnt_ref[...] * jnp.exp(sc - mx)
    pn = cntn_ref[...] * jnp.exp(sn - mx)
    den = jnp.sum(pc, axis=1, keepdims=True) + jnp.sum(pn, axis=1, keepdims=True)
    outs = []
    for h, (lo, hi) in enumerate(heads):
        r0, r1 = h * t_new, (h + 1) * t_new
        o = (_dot_nt(_bf(pc[r0:r1]), _bf(cache_ref[0, 0, 1, h]))
             + _dot(_bf(pn[r0:r1]), _bf(kvn_ref[0, :, D_A + lo:D_A + hi])))
        outs.append(o / den[r0:r1])
    o_ref[0] = jnp.concatenate(outs, axis=1)


def _sample_attention(q, kvn, cache_t, layer):
    B, T, _ = q.shape
    wb = cache_t.shape[-1]
    bias, cnt = _sample_tables(wb, T)
    per_b = lambda w: pl.BlockSpec((1, T, w), lambda b: (b, 0, 0))
    consts = [bias[:, :wb], cnt[:, :wb], bias[:, wb:], cnt[:, wb:]]
    return pl.pallas_call(
        functools.partial(_sattn_kernel, t_new=T),
        grid=(B,),
        in_specs=[per_b(D_A), per_b(2 * D_A),
                  pl.BlockSpec((1, 1, 2, H_A, HD_A, wb), lambda b: (layer, b, 0, 0, 0, 0))]
                 + [_resident(c.shape) for c in consts],
        out_specs=per_b(D_A),
        out_shape=jax.ShapeDtypeStruct((B, T, D_A), jnp.float32),
        compiler_params=pltpu.CompilerParams(dimension_semantics=("arbitrary",), vmem_limit_bytes=VMEM_LIMIT),
        name="sattn",
    )(q, kvn, cache_t, *[jnp.asarray(c) for c in consts])


def _outffn_kernel(x_ref, hm_ref, ha_ref, wo_ref, g1_ref, g2_ref, g3_ref, wg_ref, wu_ref, wd_ref, y_ref):
    ha = jnp.concatenate([ha_ref[j] for j in range(D_A // LANES)], axis=1)
    mix = _dot(_bf(hm_ref[...]), wo_ref[0:D_M, :]) + _dot(_bf(ha), wo_ref[D_M:D_M + D_A, :])
    x1 = x_ref[...] + _rms(mix, g1_ref[...])
    hf = _bf(_rms(x1, g2_ref[...]))
    gate = _dot(hf, wg_ref[...])
    up = _dot(hf, wu_ref[...])
    f = _dot(_bf(gate * jax.nn.sigmoid(gate) * up), wd_ref[...])
    y_ref[...] = x1 + _rms(f, g3_ref[...])


def _outffn(x, hm, ha, w_out, g_mix_post, g_ffn_pre, g_ffn_post, w_gate, w_up, w_down, tm):
    n, d = x.shape
    row = lambda w: pl.BlockSpec((tm, w), lambda i: (i, 0))
    return pl.pallas_call(
        _outffn_kernel,
        grid=(n // tm,),
        in_specs=[row(d), row(D_M), pl.BlockSpec((D_A // LANES, tm, LANES), lambda i: (0, i, 0)),
                  _resident(w_out.shape), _resident((1, d)), _resident((1, d)),
                  _resident((1, d)), _resident(w_gate.shape), _resident(w_up.shape), _resident(w_down.shape)],
        out_specs=row(d),
        out_shape=jax.ShapeDtypeStruct((n, d), jnp.float32),
        compiler_params=pltpu.CompilerParams(dimension_semantics=("arbitrary",), vmem_limit_bytes=VMEM_LIMIT),
        name="outffn",
    )(x, hm, ha, w_out, g_mix_post, g_ffn_pre, g_ffn_post, w_gate, w_up, w_down)


def _group_layer(x, lw, conv0, c0, n0, m0, cache, layer):
    B, T, D = x.shape
    n = B * T
    xf = x.reshape(n, D)
    qk, vm, om, qa, kva, gates = _inproj(xf, lw["g_mix_pre"], lw["w_main"], lw["w_gates"], _row_tile(n, 512))
    r3 = lambda a: a.reshape(B, T, a.shape[-1])
    hm, c_new, n_new, m_new, conv_new = _mlstm(r3(qk), r3(vm), r3(om), r3(gates), lw["b_gates"], lw["w_conv"],
                                               lw["g_mh"], conv0, c0, n0, m0, layer if cache is not None else 0)
    rows = lambda a: jnp.transpose(a, (1, 0, 2)).reshape(B, T, a.shape[0] * LANES)
    kv_rows = rows(kva)
    if cache is None:
        ha = _prompt_attention(qa, kva, B)
        kv_new = kv_rows[:, T - min(MAX_WINDOW, T):]
    else:
        ha = _sample_attention(rows(qa), kv_rows, cache, layer)
        ha = jnp.transpose(ha.reshape(n, D_A // LANES, LANES), (1, 0, 2))
        kv_new = kv_rows
    y = _outffn(xf, hm.reshape(n, D_M), ha, lw["w_out"], lw["g_mix_post"], lw["g_ffn_pre"],
                lw["g_ffn_post"], lw["w_gate"], lw["w_up"], lw["w_down"], _row_tile(n, 256))
    kv_new = kv_new.reshape(B, kv_new.shape[1], 2, H_A, HD_A)
    return (y.reshape(B, T, D), kv_new, c_new, n_new.reshape(B, H_M, HD_M), m_new[:, :, 0, 0], conv_new)


def kernel(x_prompt, x_sample, cache_kv, state_C, state_n, state_m, state_conv, g_mix_pre, g_mix_post, g_ffn_pre,
           g_ffn_post, w_in, b_gates, w_conv, g_mh, w_out, w_gate, w_up, w_down):
    f32 = jnp.float32
    depth = w_in.shape[0]
    B = x_prompt.shape[0]
    BS = x_sample.shape[0]
    hist = CONV_W - 1
    g0 = 4 * D_M
    g1 = g0 + 2 * H_M
    cache = jnp.transpose(cache_kv, (0, 1, 3, 4, 5, 2))

    zero_states = (jnp.zeros((B, hist, 2 * D_M), f32), jnp.zeros((1, B, H_M, HD_M, HD_M), f32),
                   jnp.zeros((B, H_M, 1, HD_M), f32), jnp.zeros((B, H_M, 1, LANES), f32))
    hp, hs = x_prompt, x_sample
    outs_p, outs_s = [], []
    for l in range(depth):
        lw = {
            "g_mix_pre": g_mix_pre[l][None], "g_mix_post": g_mix_post[l][None],
            "g_ffn_pre": g_ffn_pre[l][None], "g_ffn_post": g_ffn_post[l][None],
            "w_main": _bf(jnp.concatenate([w_in[l][:, :g0], w_in[l][:, g1:]], axis=1)),
            "w_gates": _bf(jnp.pad(w_in[l][:, g0:g1], ((0, 0), (0, GATE_PAD - 2 * H_M)))),
            "b_gates": jnp.pad(b_gates[l], (0, GATE_PAD - 2 * H_M))[None],
            "w_conv": w_conv[l], "g_mh": g_mh[l][None],
            "w_out": _bf(w_out[l]), "w_gate": _bf(w_gate[l]), "w_up": _bf(w_up[l]), "w_down": _bf(w_down[l]),
        }
        res = _group_layer(hp, lw, *zero_states, None, l)
        hp = res[0]
        outs_p.append(res[1:])
        m0 = jnp.broadcast_to(state_m[l][:, :, None, None], (BS, H_M, 1, LANES))
        res = _group_layer(hs, lw, state_conv[l], state_C, state_n[l][:, :, None, :], m0, cache, l)
        hs = res[0]
        outs_s.append(res[1:])
    stack = lambda outs, i: jnp.stack([o[i] for o in outs])
    return (hp, hs, stack(outs_p, 0), stack(outs_s, 0),
            stack(outs_p, 1), stack(outs_p, 2), stack(outs_p, 3), stack(outs_p, 4),
            stack(outs_s, 1), stack(outs_s, 2), stack(outs_s, 3), stack(outs_s, 4))
```

```python
import functools

import numpy as np
import jax
import jax.numpy as jnp
from jax import lax
from jax.experimental import pallas as pl
from jax.experimental.pallas import tpu as pltpu

H_M = 5
HD_M = 128
D_M = H_M * HD_M
H_A = 6
HD_A = 64
D_A = H_A * HD_A
CONV_W = 4
MLSTM_MAX_CHUNK = 256
MLSTM_GROUP_TOKENS = 256
MLSTM_MAX_GROUP = 4
DILATED_PATTERNS = ((128, 1), (512, 4), (2048, 16))
MAX_WINDOW = 2048
EPS = 1e-6
LANES = 128
GATE_PAD = LANES
ATT_BLOCK = 128
VMEM_LIMIT = 56 * 1024 * 1024

_SLOPES = [2.0 ** (-8.0 * (h + 1) / H_A) for h in range(H_A)]
_NEG_INF = float("-inf")


def _bf(x):
    return x.astype(jnp.bfloat16)


def _dot(a, b):
    return jnp.dot(a, b, preferred_element_type=jnp.float32)


def _dot_nt(a, b):
    return lax.dot_general(a, b, (((1,), (1,)), ((), ())), preferred_element_type=jnp.float32)


def _rms(x, g):
    return x * lax.rsqrt(jnp.mean(x * x, axis=-1, keepdims=True) + EPS) * g


def _resident(shape):
    nd = len(shape)
    return pl.BlockSpec(shape, lambda *_: (0,) * nd, pipeline_mode=pl.Buffered(1))


def _row_tile(n, cap):
    t = min(n, cap)
    while n % t:
        t //= 2
    return t


_IN_CUTS = (0, 2 * D_M, 3 * D_M, 4 * D_M, 4 * D_M + D_A, 4 * D_M + 3 * D_A)


def _inproj_kernel(x_ref, g_ref, w_ref, wg_ref, qk_ref, vm_ref, om_ref, qa_ref, kva_ref, gates_ref):
    a = _bf(_rms(x_ref[...], g_ref[...]))
    outs = (qk_ref, vm_ref, om_ref)
    for o_ref, lo, hi in zip(outs, _IN_CUTS[:3], _IN_CUTS[1:4]):
        o_ref[...] = _dot_nt(a, w_ref[lo:hi, :])
    for o_ref, lo, hi in zip((qa_ref, kva_ref), _IN_CUTS[3:5], _IN_CUTS[4:6]):
        z = _dot_nt(a, w_ref[lo:hi, :])
        for j in range((hi - lo) // LANES):
            o_ref[j] = z[:, j * LANES:(j + 1) * LANES]
    gates_ref[...] = _dot_nt(a, wg_ref[...])


def _inproj(x, g, w_main, w_gates, tm):
    n, d = x.shape
    row = lambda w: pl.BlockSpec((tm, w), lambda i: (i, 0))
    tiles = lambda w: pl.BlockSpec((w // LANES, tm, LANES), lambda i: (0, i, 0))
    f32 = jnp.float32
    return pl.pallas_call(
        _inproj_kernel,
        grid=(n // tm,),
        in_specs=[row(d), _resident((1, d)), _resident(w_main.shape), _resident(w_gates.shape)],
        out_specs=[row(2 * D_M), row(D_M), row(D_M), tiles(D_A), tiles(2 * D_A), row(GATE_PAD)],
        out_shape=[jax.ShapeDtypeStruct((n, 2 * D_M), f32), jax.ShapeDtypeStruct((n, D_M), f32),
                   jax.ShapeDtypeStruct((n, D_M), f32), jax.ShapeDtypeStruct((D_A // LANES, n, LANES), f32),
                   jax.ShapeDtypeStruct((2 * D_A // LANES, n, LANES), f32),
                   jax.ShapeDtypeStruct((n, GATE_PAD), f32)],
        compiler_params=pltpu.CompilerParams(dimension_semantics=("arbitrary",), vmem_limit_bytes=VMEM_LIMIT),
        name="inproj",
    )(x, g, w_main, w_gates)


_CONV_PAD = 8


def _mlstm_head(q, k, v, a_col, b, a_row, m, cmat, nvec, causal, eye):
    L = q.shape[0]
    qb, kb, vb = _bf(q), _bf(k), _bf(v)
    if a_row is None:
        a_row = jnp.sum(jnp.where(eye, a_col, 0.0), axis=0, keepdims=True)
    dlog = jnp.where(causal, b + a_row, _NEG_INF)
    inter = b + m
    mt = jnp.maximum(inter, jnp.max(dlog, axis=1, keepdims=True))
    s = _dot_nt(qb, kb) * jnp.exp(dlog - mt)
    e_inter = jnp.exp(inter - mt)
    num = _dot(_bf(s), vb) + e_inter * _dot_nt(qb, _bf(cmat))
    den = jnp.sum(s, axis=1, keepdims=True) + e_inter * jnp.sum(q * nvec, axis=1, keepdims=True)
    hh = num / jnp.maximum(jnp.abs(den), jnp.exp(-mt))

    b_last = b[L - 1:L, :]
    wlog = b_last + a_col
    m_new = jnp.maximum(b_last + m, jnp.max(wlog, axis=0, keepdims=True))
    w = jnp.exp(wlog - m_new)
    decay = jnp.exp(b_last + m - m_new)
    wv = w * v
    if L < HD_M:
        pad = jnp.zeros((HD_M - L, HD_M), jnp.float32)
        wv_t = _bf(jnp.concatenate([wv, pad], axis=0).T)
        k_pad = _bf(jnp.concatenate([k, pad], axis=0))
    else:
        wv_t = _bf(wv.T)
        k_pad = kb
    c_new = decay * cmat + _dot(wv_t, k_pad)
    n_new = decay * nvec + jnp.sum(w * k, axis=0, keepdims=True)
    return hh, c_new, n_new, m_new


def _mlstm_kernel(qk_ref, v_ref, o_ref, g_ref, bias_ref, wconv_ref, gmh_ref, conv0_ref, c0_ref, n0_ref, m0_ref,
                  h_ref, c_out, n_out, m_out, conv_out, ubuf, c_s, n_s, m_s, *, chunk, group):
    L = chunk
    c = pl.program_id(1)
    hist = CONV_W - 1

    @pl.when(c == 0)
    def _():
        ubuf[:, _CONV_PAD - hist:_CONV_PAD, :] = conv0_ref[...]
        c_s[...] = c0_ref[0]
        n_s[...] = n0_ref[...]
        m_s[...] = m0_ref[...]

    row_id = lax.broadcasted_iota(jnp.int32, (L, GATE_PAD), 0)
    ti = lax.broadcasted_iota(jnp.int32, (L, L), 0)
    si = lax.broadcasted_iota(jnp.int32, (L, L), 1)
    causal = si <= ti
    eye = si == ti

    for g in range(group):
        ubuf[g, _CONV_PAD:_CONV_PAD + L, :] = qk_ref[g]
        conv = ubuf[g, _CONV_PAD - hist:_CONV_PAD - hist + L, :] * wconv_ref[0:1, :]
        for i in range(1, CONV_W):
            conv = conv + ubuf[g, _CONV_PAD - hist + i:_CONV_PAD - hist + i + L, :] * wconv_ref[i:i + 1, :]
        tail = ubuf[g, _CONV_PAD + L - hist:_CONV_PAD + L, :]
        ubuf[g, _CONV_PAD - hist:_CONV_PAD, :] = tail
        conv_out[g] = tail
        qk = conv * jax.nn.sigmoid(conv)

        gates = g_ref[g] + bias_ref[...]
        lf = jnp.minimum(gates, 0.0) - jnp.log(1.0 + jnp.exp(-jnp.abs(gates)))
        bcum = lf
        step = 1
        while step < L:
            bcum = bcum + jnp.where(row_id >= step, pltpu.roll(bcum, step, 0), 0.0)
            step *= 2

        a_all = gates - pltpu.roll(bcum, GATE_PAD - H_M, 1)
        a_rows = a_all.T if L % LANES == 0 else None

        for h in range(H_M):
            lanes = slice(h * HD_M, (h + 1) * HD_M)
            hh, c_new, n_new, m_new = _mlstm_head(
                qk[:, lanes], qk[:, D_M + h * HD_M:D_M + (h + 1) * HD_M] * (HD_M ** -0.5), v_ref[g, :, lanes],
                a_all[:, h:h + 1], bcum[:, H_M + h:H_M + h + 1], None if a_rows is None else a_rows[h:h + 1, :],
                m_s[g, h][:, 0:1], c_s[g, h], n_s[g, h], causal, eye)
            c_s[g, h] = c_new
            n_s[g, h] = n_new
            m_s[g, h] = jnp.broadcast_to(m_new, (1, LANES))
            h_ref[g, :, lanes] = jax.nn.sigmoid(o_ref[g, :, lanes]) * _rms(hh, gmh_ref[:, lanes])

    @pl.when(c == pl.num_programs(1) - 1)
    def _():
        c_out[...] = c_s[...]
        n_out[...] = n_s[...]
        m_out[...] = m_s[...]


def _mlstm_tiling(B, T):
    chunk = _row_tile(T, MLSTM_MAX_CHUNK)
    group = _row_tile(B, max(1, min(MLSTM_MAX_GROUP, MLSTM_GROUP_TOKENS // chunk)))
    return chunk, group


def _mlstm(qk, vm, om, gates, b_gates, w_conv, g_mh, conv0, c0, n0, m0, layer):
    B, T, _ = qk.shape
    chunk, G = _mlstm_tiling(B, T)
    tok = lambda w: pl.BlockSpec((G, chunk, w), lambda b, c: (b, c, 0))
    per_b = lambda shape: pl.BlockSpec((G,) + shape, lambda b, c: (b,) + (0,) * len(shape))
    hist = CONV_W - 1
    f32 = jnp.float32
    return pl.pallas_call(
        functools.partial(_mlstm_kernel, chunk=chunk, group=G),
        grid=(B // G, T // chunk),
        in_specs=[tok(2 * D_M), tok(D_M), tok(D_M), tok(GATE_PAD),
                  _resident((1, GATE_PAD)), _resident((CONV_W, 2 * D_M)), _resident((1, D_M)),
                  per_b((hist, 2 * D_M)),
                  pl.BlockSpec((1, G, H_M, HD_M, HD_M), lambda b, c: (layer, b, 0, 0, 0)),
                  per_b((H_M, 1, HD_M)), per_b((H_M, 1, LANES))],
        out_specs=[tok(D_M), per_b((H_M, HD_M, HD_M)), per_b((H_M, 1, HD_M)), per_b((H_M, 1, LANES)),
                   per_b((hist, 2 * D_M))],
        out_shape=[jax.ShapeDtypeStruct((B, T, D_M), f32),
                   jax.ShapeDtypeStruct((B, H_M, HD_M, HD_M), f32),
                   jax.ShapeDtypeStruct((B, H_M, 1, HD_M), f32),
                   jax.ShapeDtypeStruct((B, H_M, 1, LANES), f32),
                   jax.ShapeDtypeStruct((B, hist, 2 * D_M), f32)],
        scratch_shapes=[pltpu.VMEM((G, _CONV_PAD + chunk, 2 * D_M), f32),
                        pltpu.VMEM((G, H_M, HD_M, HD_M), f32),
                        pltpu.VMEM((G, H_M, 1, HD_M), f32),
                        pltpu.VMEM((G, H_M, 1, LANES), f32)],
        compiler_params=pltpu.CompilerParams(dimension_semantics=("arbitrary", "arbitrary"),
                                             vmem_limit_bytes=VMEM_LIMIT),
        name="mlstm",
    )(qk, vm, om, gates, b_gates, w_conv, g_mh, conv0, c0, n0, m0)


SUBLANES = 8


def _mlstm_short_kernel(qk_ref, v_ref, o_ref, g_ref, hist_ref, bias_ref, wconv_ref, gmh_ref, c0_ref, n0_ref, m0_ref,
                        h_ref, c_out, n_out, m_out, qx, kx, *, group):
    G, T = group, SUBLANES
    R = G * T
    f32 = jnp.float32

    @pl.when(pl.program_id(0) == 0)
    def _():
        qx[...] = jnp.zeros(qx.shape, f32)
        kx[...] = jnp.zeros(kx.shape, f32)

    grp = lambda x: x.reshape(G, T, x.shape[-1])
    rows_of = lambda x: jnp.broadcast_to(x[:, None, :], (G, T, x.shape[-1])).reshape(R, x.shape[-1])
    shift_gate = lambda x: pltpu.roll(x, GATE_PAD - H_M, 1)

    t_wide = lax.broadcasted_iota(jnp.int32, (R, 2 * D_M), 0) % T
    u = qk_ref[...]
    hist = hist_ref[...]
    conv = u * wconv_ref[CONV_W - 1:CONV_W, :]
    for k in range(1, CONV_W):
        back = jnp.where(t_wide >= k, pltpu.roll(u, k, 0), pltpu.roll(hist, (R - (CONV_W - 1 - k)) % R, 0))
        conv = conv + back * wconv_ref[CONV_W - 1 - k:CONV_W - k, :]
    qk = conv * jax.nn.sigmoid(conv)

    gates = g_ref[...] + bias_ref[...]
    lf = jnp.minimum(gates, 0.0) - jnp.log(1.0 + jnp.exp(-jnp.abs(gates)))
    t_id = lax.broadcasted_iota(jnp.int32, (R, GATE_PAD), 0) % T
    bcum = lf
    step = 1
    while step < T:
        bcum = bcum + jnp.where(t_id >= step, pltpu.roll(bcum, step, 0), 0.0)
        step *= 2

    a_all = gates - shift_gate(bcum)
    a_rows = a_all.T
    b_last = shift_gate(rows_of(grp(bcum)[:, T - 1, :]))
    m_rows = rows_of(m0_ref[...])
    wlog = b_last + a_all
    m_new = jnp.maximum(b_last + m_rows, rows_of(jnp.max(grp(wlog), axis=1)))
    w_all = jnp.exp(wlog - m_new)
    decay = grp(jnp.exp(b_last + m_rows - m_new))[:, 0, :]
    m_out[...] = grp(m_new)[:, 0, :]

    ri = lax.broadcasted_iota(jnp.int32, (R, R), 0)
    ci = lax.broadcasted_iota(jnp.int32, (R, R), 1)
    mask = (ri // T == ci // T) & (ci <= ri)

    for h in range(H_M):
        lanes = slice(h * HD_M, (h + 1) * HD_M)
        q = qk[:, lanes]
        k = qk[:, D_M + h * HD_M:D_M + (h + 1) * HD_M] * (HD_M ** -0.5)
        v = v_ref[:, lanes]
        qb, kb, vb = _bf(q), _bf(k), _bf(v)
        b = bcum[:, H_M + h:H_M + h + 1]
        m = m_rows[:, h:h + 1]
        dlog = jnp.where(mask, b + a_rows[h:h + 1, :], _NEG_INF)
        inter = b + m
        mt = jnp.maximum(inter, jnp.max(dlog, axis=1, keepdims=True))
        s = _dot_nt(qb, kb) * jnp.exp(dlog - mt)
        e_inter = jnp.exp(inter - mt)
        for g in range(G):
            qx[g * T:(g + 1) * T, g * HD_M:(g + 1) * HD_M] = q[g * T:(g + 1) * T, :]
            kx[g * T:(g + 1) * T, g * HD_M:(g + 1) * HD_M] = k[g * T:(g + 1) * T, :]
        c_cat = jnp.concatenate([c0_ref[0, g, h] for g in range(G)], axis=1)
        n_h = n0_ref[h]
        num = _dot(_bf(s), vb) + e_inter * _dot_nt(_bf(qx[...]), _bf(c_cat))
        den = jnp.sum(s, axis=1, keepdims=True) + e_inter * jnp.sum(q * rows_of(n_h), axis=1, keepdims=True)
        hh = num / jnp.maximum(jnp.abs(den), jnp.exp(-mt))
        h_ref[:, lanes] = jax.nn.sigmoid(o_ref[:, lanes]) * _rms(hh, gmh_ref[:, lanes])

        w = w_all[:, h:h + 1]
        upd = _dot(_bf((w * v).T), _bf(kx[...]))
        for g in range(G):
            c_out[g, h] = decay[g:g + 1, h:h + 1] * c0_ref[0, g, h] + upd[:, g * HD_M:(g + 1) * HD_M]
        n_out[h] = decay[:, h:h + 1] * n_h + jnp.sum(grp(w * k), axis=1)


def _mlstm_short(qk, vm, om, gates, b_gates, w_conv, g_mh, hist, c0, n0, m0, layer, B):
    T = SUBLANES
    G = LANES // T
    assert B % G == 0 and qk.shape[0] == B * T
    R = G * T
    row = lambda w: pl.BlockSpec((R, w), lambda i: (i, 0))
    f32 = jnp.float32
    return pl.pallas_call(
        functools.partial(_mlstm_short_kernel, group=G),
        grid=(B // G,),
        in_specs=[row(2 * D_M), row(D_M), row(D_M), row(GATE_PAD), row(2 * D_M),
                  _resident((1, GATE_PAD)), _resident((CONV_W, 2 * D_M)), _resident((1, D_M)),
                  pl.BlockSpec((1, G, H_M, HD_M, HD_M), lambda i: (layer, i, 0, 0, 0)),
                  pl.BlockSpec((H_M, G, HD_M), lambda i: (0, i, 0)),
                  pl.BlockSpec((G, GATE_PAD), lambda i: (i, 0))],
        out_specs=[row(D_M), pl.BlockSpec((G, H_M, HD_M, HD_M), lambda i: (i, 0, 0, 0)),
                   pl.BlockSpec((H_M, G, HD_M), lambda i: (0, i, 0)), pl.BlockSpec((G, GATE_PAD), lambda i: (i, 0))],
        out_shape=[jax.ShapeDtypeStruct((B * T, D_M), f32), jax.ShapeDtypeStruct((B, H_M, HD_M, HD_M), f32),
                   jax.ShapeDtypeStruct((H_M, B, HD_M), f32), jax.ShapeDtypeStruct((B, GATE_PAD), f32)],
        scratch_shapes=[pltpu.VMEM((R, G * HD_M), f32), pltpu.VMEM((R, G * HD_M), f32)],
        compiler_params=pltpu.CompilerParams(dimension_semantics=("arbitrary",), vmem_limit_bytes=VMEM_LIMIT),
        name="mlstm_short",
    )(qk, vm, om, gates, hist, b_gates, w_conv, g_mh, c0, n0, m0)


ATT_TILE =ATT_BLOCK * max(d for _, d in DILATED_PATTERNS)


def _pattn_kernel(q_ref, kv_ref, o_ref, kvt_ref, prev_s, acc_s, m_s, l_s, s_buf, p_buf, m_buf):
    i = pl.program_id(1)
    nb = ATT_BLOCK
    pair = 2 * HD_A
    scale = HD_A ** -0.5

    @pl.when(i == 0)
    def _():
        prev_s[...] = jnp.zeros(prev_s.shape, prev_s.dtype)

    qi = lax.broadcasted_iota(jnp.int32, (nb, nb), 0)
    ki = lax.broadcasted_iota(jnp.int32, (nb, nb), 1)
    dist = qi - ki
    lo_half = lax.broadcasted_iota(jnp.int32, (nb, pair), 1) < HD_A
    ones = jnp.ones((2 * nb, pair), jnp.bfloat16)
    no_keys = jnp.full((nb, nb), _NEG_INF, jnp.float32)
    npat = len(DILATED_PATTERNS)
    npair = H_A // 2

    def rows_of(start, dil):
        return pl.ds(start, nb, stride=dil) if dil > 1 else pl.ds(start, nb)

    for pi, (win, dil) in enumerate(DILATED_PATTERNS):
        first, last = pi == 0, pi == npat - 1
        span = dil * nb
        nblk = ATT_TILE // span
        nd_cur = jnp.where(dist >= 0, -(dist * dil).astype(jnp.float32), _NEG_INF)
        nd_prev = jnp.where(dist <= 0, -((dist + nb) * dil).astype(jnp.float32), _NEG_INF)
        nd_prev_first = jnp.where(i > 0, nd_prev, no_keys)

        def unit(start, prev_ref, prev_start, nd_prev_u, dil=dil, first=first, last=last, nd_cur=nd_cur):
            rows = rows_of(start, dil)
            prows = rows_of(prev_start, dil)
            nd = jnp.concatenate([nd_cur, nd_prev_u], axis=1)
            for hp in range(npair):
                q2 = q_ref[hp, rows, :] * scale
                k2 = _bf(jnp.concatenate([kv_ref[hp, rows, :], prev_ref[hp, prows, :]], axis=0))
                for e in range(2):
                    qm = _bf(jnp.where(lo_half if e == 0 else jnp.logical_not(lo_half), q2, 0.0))
                    s_buf[2 * hp + e] = _dot_nt(qm, k2) + _SLOPES[2 * hp + e] * nd
            for h in range(H_A):
                s = s_buf[h]
                m = jnp.max(s, axis=1, keepdims=True)
                p_buf[h] = _bf(jnp.exp(s - m))
                m_buf[h] = jnp.broadcast_to(m, (nb, pair))
            for hp in range(npair):
                v2 = _bf(jnp.concatenate([kv_ref[npair + hp, rows, :], prev_ref[npair + hp, prows, :]], axis=0))
                rhs = jnp.concatenate([v2, ones], axis=1)
                r_e = _dot(p_buf[2 * hp], rhs)
                r_o = _dot(p_buf[2 * hp + 1], rhs)
                acc2 = jnp.where(lo_half, r_e[:, :pair], r_o[:, :pair])
                l2 = jnp.where(lo_half, r_e[:, pair:], r_o[:, pair:])
                m2 = jnp.where(lo_half, m_buf[2 * hp], m_buf[2 * hp + 1])
                if not first:
                    m_old = m_s[hp, rows, :]
                    m_new = jnp.maximum(m_old, m2)
                    a_old = jnp.exp(m_old - m_new)
                    a_loc = jnp.exp(m2 - m_new)
                    acc2 = a_old * acc_s[hp, rows, :] + a_loc * acc2
                    l2 = a_old * l_s[hp, rows, :] + a_loc * l2
                    m2 = m_new
                if last:
                    o_ref[hp, rows, :] = acc2 / l2
                else:
                    acc_s[hp, rows, :] = acc2
                    m_s[hp, rows, :] = m2
                    l_s[hp, rows, :] = l2

        def first_block(c, carry, nblk=nblk, span=span, unit=unit, nd_prev_first=nd_prev_first):
            unit(c, prev_s, c + (nblk - 1) * span, nd_prev_first)
            return carry

        def later_block(u, carry, dil=dil, span=span, unit=unit, nd_prev=nd_prev):
            start = u % dil + (1 + u // dil) * span
            unit(start, kv_ref, start - span, nd_prev)
            return carry

        lax.fori_loop(0, dil, first_block, 0)
        if nblk > 1:
            lax.fori_loop(0, dil * (nblk - 1), later_block, 0)

    prev_s[...] = kv_ref[...]

    @pl.when(i == pl.num_programs(1) - 1)
    def _():
        for j in range(2 * npair):
            t = kv_ref[j].T
            kvt_ref[0, j // npair, 2 * (j % npair)] = t[0:HD_A]
            kvt_ref[0, j // npair, 2 * (j % npair) + 1] = t[HD_A:pair]


def _prompt_attention(q, kv, B):
    nq, n, _ = q.shape
    S = n // B
    for win, dil in DILATED_PATTERNS:
        assert win // dil == ATT_BLOCK and ATT_TILE % (dil * ATT_BLOCK) == 0
    assert S % ATT_TILE == 0 and LANES == 2 * HD_A
    nt = S // ATT_TILE
    tile = lambda k: pl.BlockSpec((k, ATT_TILE, LANES), lambda b, i: (0, b * nt + i, 0))
    f32 = jnp.float32
    return pl.pallas_call(
        _pattn_kernel,
        grid=(B, nt),
        in_specs=[tile(nq), tile(2 * nq)],
        out_specs=[tile(nq), pl.BlockSpec((1, 2, H_A, HD_A, ATT_TILE), lambda b, i: (b, 0, 0, 0, 0))],
        out_shape=[jax.ShapeDtypeStruct((nq, n, LANES), f32),
                   jax.ShapeDtypeStruct((B, 2, H_A, HD_A, ATT_TILE), f32)],
        scratch_shapes=[pltpu.VMEM((2 * nq, ATT_TILE, LANES), f32), pltpu.VMEM((nq, ATT_TILE, LANES), f32),
                        pltpu.VMEM((nq, ATT_TILE, LANES), f32), pltpu.VMEM((nq, ATT_TILE, LANES), f32),
                        pltpu.VMEM((H_A, ATT_BLOCK, 2 * ATT_BLOCK), f32),
                        pltpu.VMEM((H_A, ATT_BLOCK, 2 * ATT_BLOCK), jnp.bfloat16),
                        pltpu.VMEM((H_A, ATT_BLOCK, LANES), f32)],
        compiler_params=pltpu.CompilerParams(dimension_semantics=("arbitrary", "arbitrary"),
                                             vmem_limit_bytes=VMEM_LIMIT),
        name="pattn",
    )(q, kv)


def _sample_tables(wb, t_new):
    rows = np.arange(H_A * t_new)
    t = rows % t_new
    slope = np.asarray(_SLOPES)[rows // t_new]
    pos = np.arange(wb + t_new)
    d = wb + t[:, None] - pos[None, :]
    cnt = np.zeros(d.shape, np.float32)
    for win, dil in DILATED_PATTERNS:
        cnt += ((d >= 0) & (d % dil == 0) & (d <= win)).astype(np.float32)
    bias = np.where(cnt > 0, -slope[:, None] * d, -np.inf).astype(np.float32)
    return bias, cnt


def _sattn_kernel(q_ref, kvn_ref, cache_ref, bias_ref, cnt_ref, biasn_ref, cntn_ref, o_ref, *, t_new):
    scale = HD_A ** -0.5
    heads = [(h * HD_A, (h + 1) * HD_A) for h in range(H_A)]
    qh = [_bf(q_ref[0, :, lo:hi]) for lo, hi in heads]
    sc = jnp.concatenate([_dot(qh[h], _bf(cache_ref[0, 0, 0, h])) for h in range(H_A)], axis=0)
    sn = jnp.concatenate([_dot_nt(qh[h], _bf(kvn_ref[0, :, lo:hi])) for h, (lo, hi) in enumerate(heads)], axis=0)
    sc = sc * scale + bias_ref[...]
    sn = sn * scale + biasn_ref[...]
    mx = jnp.maximum(jnp.max(sc, axis=1, keepdims=True), jnp.max(sn, axis=1, keepdims=True))
    pc = cnt_ref[...] * jnp.exp(sc - mx)
    pn = cntn_ref[...] * jnp.exp(sn - mx)
    den = jnp.sum(pc, axis=1, keepdims=True) + jnp.sum(pn, axis=1, keepdims=True)
    outs = []
    for h, (lo, hi) in enumerate(heads):
        r0, r1 = h * t_new, (h + 1) * t_new
        o = (_dot_nt(_bf(pc[r0:r1]), _bf(cache_ref[0, 0, 1, h]))
             + _dot(_bf(pn[r0:r1]), _bf(kvn_ref[0, :, D_A + lo:D_A + hi])))
        outs.append(o / den[r0:r1])
    o_ref[0] = jnp.concatenate(outs, axis=1)


def _sample_attention(q, kvn, cache_t, layer):
    B, T, _ = q.shape
    wb = cache_t.shape[-1]
    bias, cnt = _sample_tables(wb, T)
    per_b = lambda w: pl.BlockSpec((1, T, w), lambda b: (b, 0, 0))
    consts = [bias[:, :wb], cnt[:, :wb], bias[:, wb:], cnt[:, wb:]]
    return pl.pallas_call(
        functools.partial(_sattn_kernel, t_new=T),
        grid=(B,),
        in_specs=[per_b(D_A), per_b(2 * D_A),
                  pl.BlockSpec((1, 1, 2, H_A, HD_A, wb), lambda b: (layer, b, 0, 0, 0, 0))]
                 + [_resident(c.shape) for c in consts],
        out_specs=per_b(D_A),
        out_shape=jax.ShapeDtypeStruct((B, T, D_A), jnp.float32),
        compiler_params=pltpu.CompilerParams(dimension_semantics=("arbitrary",), vmem_limit_bytes=VMEM_LIMIT),
        name="sattn",
    )(q, kvn, cache_t, *[jnp.asarray(c) for c in consts])


def _outffn_kernel(x_ref, hm_ref, ha_ref, wo_ref, g1_ref, g2_ref, g3_ref, wg_ref, wu_ref, wd_ref, y_ref):
    ha = jnp.concatenate([ha_ref[j] for j in range(D_A // LANES)], axis=1)
    mix = _dot(_bf(hm_ref[...]), wo_ref[0:D_M, :]) + _dot(_bf(ha), wo_ref[D_M:D_M + D_A, :])
    x1 = x_ref[...] + _rms(mix, g1_ref[...])
    hf = _bf(_rms(x1, g2_ref[...]))
    gate = _dot(hf, wg_ref[...])
    up = _dot(hf, wu_ref[...])
    f = _dot(_bf(gate * jax.nn.sigmoid(gate) * up), wd_ref[...])
    y_ref[...] = x1 + _rms(f, g3_ref[...])


def _outffn(x, hm, ha, w_out, g_mix_post, g_ffn_pre, g_ffn_post, w_gate, w_up, w_down, tm):
    n, d = x.shape
    row = lambda w: pl.BlockSpec((tm, w), lambda i: (i, 0))
    return pl.pallas_call(
        _outffn_kernel,
        grid=(n // tm,),
        in_specs=[row(d), row(D_M), pl.BlockSpec((D_A // LANES, tm, LANES), lambda i: (0, i, 0)),
                  _resident(w_out.shape), _resident((1, d)), _resident((1, d)),
                  _resident((1, d)), _resident(w_gate.shape), _resident(w_up.shape), _resident(w_down.shape)],
        out_specs=row(d),
        out_shape=jax.ShapeDtypeStruct((n, d), jnp.float32),
        compiler_params=pltpu.CompilerParams(dimension_semantics=("arbitrary",), vmem_limit_bytes=VMEM_LIMIT),
        name="outffn",
    )(x, hm, ha, w_out, g_mix_post, g_ffn_pre, g_ffn_post, w_gate, w_up, w_down)


def _group_layer(x, lw, conv0, c0, n0, m0, cache, layer):
    B, T, D = x.shape
    n = B * T
    xf = x.reshape(n, D)
    qk, vm, om, qa, kva, gates = _inproj(xf, lw["g_mix_pre"], lw["w_main"], lw["w_gates"], _row_tile(n, 1024))
    r3 = lambda a: a.reshape(B, T, a.shape[-1])
    state_layer = layer if cache is not None else 0
    hist = CONV_W - 1
    if T == SUBLANES and B % (LANES // SUBLANES) == 0:
        conv_rows = jnp.pad(conv0, ((0, 0), (0, T - hist), (0, 0))).reshape(n, 2 * D_M)
        hm, c_new, n_new, m_new = _mlstm_short(
            qk, vm, om, gates, lw["b_gates"], lw["w_conv"], lw["g_mh"], conv_rows, c0,
            jnp.transpose(n0, (1, 0, 2)), jnp.pad(m0, ((0, 0), (0, GATE_PAD - H_M))), state_layer, B)
        n_new = jnp.transpose(n_new, (1, 0, 2))
        m_new = m_new[:, :H_M]
        conv_new = r3(qk)[:, T - hist:]
    else:
        hm, c_new, n_new, m_new, conv_new = _mlstm(
            r3(qk), r3(vm), r3(om), r3(gates), lw["b_gates"], lw["w_conv"], lw["g_mh"], conv0, c0,
            n0[:, :, None, :], jnp.broadcast_to(m0[:, :, None, None], (B, H_M, 1, LANES)), state_layer)
        n_new = n_new.reshape(B, H_M, HD_M)
        m_new = m_new[:, :, 0, 0]
    rows = lambda a: jnp.transpose(a, (1, 0, 2)).reshape(B, T, a.shape[0] * LANES)
    if cache is None:
        assert MAX_WINDOW == ATT_TILE
        ha, kv_t = _prompt_attention(qa, kva, B)
        kv_new = jnp.transpose(kv_t, (0, 4, 1, 2, 3))
    else:
        kv_rows = rows(kva)
        ha = _sample_attention(rows(qa), kv_rows, cache, layer)
        ha = jnp.transpose(ha.reshape(n, D_A // LANES, LANES), (1, 0, 2))
        kv_new = kv_rows.reshape(B, T, 2, H_A, HD_A)
    y = _outffn(xf, hm.reshape(n, D_M), ha, lw["w_out"], lw["g_mix_post"], lw["g_ffn_pre"],
                lw["g_ffn_post"], lw["w_gate"], lw["w_up"], lw["w_down"], _row_tile(n, 512))
    return (y.reshape(B, T, D), kv_new, c_new, n_new, m_new, conv_new)


def kernel(x_prompt, x_sample, cache_kv, state_C, state_n, state_m, state_conv, g_mix_pre, g_mix_post, g_ffn_pre,
           g_ffn_post, w_in, b_gates, w_conv, g_mh, w_out, w_gate, w_up, w_down):
    f32 = jnp.float32
    depth = w_in.shape[0]
    B = x_prompt.shape[0]
    hist = CONV_W - 1
    g0 = 4 * D_M
    g1 = g0 + 2 * H_M
    cache = jnp.transpose(cache_kv, (0, 1, 3, 4, 5, 2))
    w_in_t = _bf(jnp.transpose(w_in, (0, 2, 1)))

    zero_states = (jnp.zeros((B, hist, 2 * D_M), f32), jnp.zeros((1, B, H_M, HD_M, HD_M), f32),
                   jnp.zeros((B, H_M, HD_M), f32), jnp.zeros((B, H_M), f32))
    hp, hs = x_prompt, x_sample
    outs_p, outs_s = [], []
    for l in range(depth):
        lw = {
            "g_mix_pre": g_mix_pre[l][None], "g_mix_post": g_mix_post[l][None],
            "g_ffn_pre": g_ffn_pre[l][None], "g_ffn_post": g_ffn_post[l][None],
            "w_main": jnp.concatenate([w_in_t[l, :g0], w_in_t[l, g1:]], axis=0),
            "w_gates": jnp.pad(w_in_t[l, g0:g1], ((0, GATE_PAD - 2 * H_M), (0, 0))),
            "b_gates": jnp.pad(b_gates[l], (0, GATE_PAD - 2 * H_M))[None],
            "w_conv": w_conv[l], "g_mh": g_mh[l][None],
            "w_out": _bf(w_out[l]), "w_gate": _bf(w_gate[l]), "w_up": _bf(w_up[l]), "w_down": _bf(w_down[l]),
        }
        res = _group_layer(hp, lw, *zero_states, None, l)
        hp = res[0]
        outs_p.append(res[1:])
        res = _group_layer(hs, lw, state_conv[l], state_C, state_n[l], state_m[l], cache, l)
        hs = res[0]
        outs_s.append(res[1:])
    stack = lambda outs, i: jnp.stack([o[i] for o in outs])
    return (hp, hs, stack(outs_p, 0), stack(outs_s, 0),
            stack(outs_p, 1), stack(outs_p, 2), stack(outs_p, 3), stack(outs_p, 4),
            stack(outs_s, 1), stack(outs_s, 2), stack(outs_s, 3), stack(outs_s, 4))
```

```python
import functools

import numpy as np
import jax
import jax.numpy as jnp
from jax import lax
from jax.experimental import pallas as pl
from jax.experimental.pallas import tpu as pltpu

H_M = 5
HD_M = 128
D_M = H_M * HD_M
H_A = 6
HD_A = 64
D_A = H_A * HD_A
CONV_W = 4
MLSTM_MAX_CHUNK = 256
MLSTM_GROUP_TOKENS = 256
MLSTM_MAX_GROUP = 4
DILATED_PATTERNS = ((128, 1), (512, 4), (2048, 16))
MAX_WINDOW = 2048
EPS = 1e-6
LANES = 128
GATE_PAD = LANES
INPROJ_LONG_ROWS = 256
SUBLANES = 8
ATT_BLOCK = 128
ATT_STRIDE = max(d for _, d in DILATED_PATTERNS)
ATT_TILE = ATT_BLOCK * ATT_STRIDE
VMEM_LIMIT = 56 * 1024 * 1024

_SLOPES = [2.0 ** (-8.0 * (h + 1) / H_A) for h in range(H_A)]
_NEG_INF = float("-inf")


def _bf(x):
    return x.astype(jnp.bfloat16)


def _dot(a, b):
    return jnp.dot(a, b, preferred_element_type=jnp.float32)


def _dot_nt(a, b):
    return lax.dot_general(a, b, (((1,), (1,)), ((), ())), preferred_element_type=jnp.float32)


def _rms(x, g):
    return x * lax.rsqrt(jnp.mean(x * x, axis=-1, keepdims=True) + EPS) * g


def _resident(shape):
    nd = len(shape)
    return pl.BlockSpec(shape, lambda *_: (0,) * nd, pipeline_mode=pl.Buffered(1))


def _row_tile(n, cap):
    t = min(n, cap)
    while n % t:
        t //= 2
    return t


_IN_CUTS = (0, 2 * D_M, 3 * D_M, 4 * D_M, 4 * D_M + D_A, 4 * D_M + 3 * D_A)


_CONV_PAD = 8


def _sample_attention_rows(q_ref, kvn_ref, cache_ref, bias_ref, cnt_ref, biasn_ref, cntn_ref, o_ref, b):
    t_new = q_ref.shape[1]
    scale = HD_A ** -0.5
    heads = [(h * HD_A, (h + 1) * HD_A) for h in range(H_A)]
    qh = [_bf(q_ref[b, :, lo:hi]) for lo, hi in heads]
    sc = jnp.concatenate([_dot(qh[h], _bf(cache_ref[0, b, 0, h])) for h in range(H_A)], axis=0)
    sn = jnp.concatenate([_dot_nt(qh[h], _bf(kvn_ref[b, :, lo:hi])) for h, (lo, hi) in enumerate(heads)], axis=0)
    sc = sc * scale + bias_ref[...]
    sn = sn * scale + biasn_ref[...]
    mx = jnp.maximum(jnp.max(sc, axis=1, keepdims=True), jnp.max(sn, axis=1, keepdims=True))
    pc = cnt_ref[...] * jnp.exp(sc - mx)
    pn = cntn_ref[...] * jnp.exp(sn - mx)
    den = jnp.sum(pc, axis=1, keepdims=True) + jnp.sum(pn, axis=1, keepdims=True)
    outs = []
    for h, (lo, hi) in enumerate(heads):
        r0, r1 = h * t_new, (h + 1) * t_new
        o = (_dot_nt(_bf(pc[r0:r1]), _bf(cache_ref[0, b, 1, h]))
             + _dot(_bf(pn[r0:r1]), _bf(kvn_ref[b, :, D_A + lo:D_A + hi])))
        outs.append(o / den[r0:r1])
    o_ref[b] = jnp.concatenate(outs, axis=1)


def _inproj_kernel(*refs, tm, guests):
    if not guests:
        x_ref, g_ref, w_ref, wg_ref, qk_ref, vm_ref, om_ref, qa_ref, kva_ref, gates_ref = refs
    else:
        (x_ref, g_ref, w_ref, wg_ref, sq_ref, skv_ref, cache_ref, bias_ref, cnt_ref, biasn_ref, cntn_ref,
         qk_ref, vm_ref, om_ref, qa_ref, kva_ref, gates_ref, sha_ref, zs) = refs
    a = _bf(_rms(x_ref[...], g_ref[...]))
    nq = D_A // LANES
    att_tile = lambda j: (qa_ref, j) if j < nq else (kva_ref, j - nq)
    z = _dot_nt(a, w_ref[_IN_CUTS[3]:_IN_CUTS[5], :])
    if not guests:
        for j in range(3 * nq):
            dst, jj = att_tile(j)
            dst[jj] = z[:, j * LANES:(j + 1) * LANES]
    else:
        for j in range(3 * nq):
            zs[j] = z[:, j * LANES:(j + 1) * LANES]
    qk_ref[...] = _dot_nt(a, w_ref[_IN_CUTS[0]:_IN_CUTS[1], :])
    if guests:
        sub = pl.program_id(0) % (ATT_TILE // tm)
        per = tm // ATT_STRIDE
        for j in range(3 * nq):
            dst, jj = att_tile(j)
            for r in range(ATT_STRIDE):
                start = pl.multiple_of(r * ATT_BLOCK + sub * per, SUBLANES)
                dst[jj, pl.ds(start, per), :] = zs[j, pl.ds(r, per, stride=ATT_STRIDE), :]
    vm_ref[...] = _dot_nt(a, w_ref[_IN_CUTS[1]:_IN_CUTS[2], :])
    for b in range(guests):
        _sample_attention_rows(sq_ref, skv_ref, cache_ref, bias_ref, cnt_ref, biasn_ref, cntn_ref, sha_ref, b)
    om_ref[...] = _dot_nt(a, w_ref[_IN_CUTS[2]:_IN_CUTS[3], :])
    gates_ref[...] = _dot_nt(a, wg_ref[...])


def _inproj(x, g, w_main, w_gates, tm, guest=None):
    n, d = x.shape
    row = lambda w: pl.BlockSpec((tm, w), lambda i: (i, 0))
    f32 = jnp.float32
    in_specs = [row(d), _resident((1, d)), _resident(w_main.shape), _resident(w_gates.shape)]
    out_shape = [jax.ShapeDtypeStruct((n, 2 * D_M), f32), jax.ShapeDtypeStruct((n, D_M), f32),
                 jax.ShapeDtypeStruct((n, D_M), f32), jax.ShapeDtypeStruct((D_A // LANES, n, LANES), f32),
                 jax.ShapeDtypeStruct((2 * D_A // LANES, n, LANES), f32), jax.ShapeDtypeStruct((n, GATE_PAD), f32)]
    args = [x, g, w_main, w_gates]
    scratch = []
    guests = 0
    if guest is None:
        tiles = lambda w: pl.BlockSpec((w // LANES, tm, LANES), lambda i: (0, i, 0))
        out_specs = [row(2 * D_M), row(D_M), row(D_M), tiles(D_A), tiles(2 * D_A), row(GATE_PAD)]
    else:
        sq, skv, cache_t, layer = guest
        BS, T, _ = sq.shape
        wb = cache_t.shape[-1]
        steps = n // tm
        assert n % ATT_TILE == 0 and ATT_TILE % tm == 0 and tm % (ATT_STRIDE * SUBLANES) == 0 and BS % steps == 0
        guests = BS // steps
        per_att = ATT_TILE // tm
        tiles = lambda w: pl.BlockSpec((w // LANES, ATT_TILE, LANES), lambda i: (0, i // per_att, 0),
                                       pipeline_mode=pl.Buffered(1))
        bias, cnt = _sample_tables(wb, T)
        consts = [bias[:, :wb], cnt[:, :wb], bias[:, wb:], cnt[:, wb:]]
        per_g = lambda w: pl.BlockSpec((guests, T, w), lambda i: (i, 0, 0))
        in_specs += [per_g(D_A), per_g(2 * D_A),
                     pl.BlockSpec((1, guests, 2, H_A, HD_A, wb), lambda i: (layer, i, 0, 0, 0, 0))]
        in_specs += [_resident(c.shape) for c in consts]
        out_specs = [row(2 * D_M), row(D_M), row(D_M), tiles(D_A), tiles(2 * D_A), row(GATE_PAD), per_g(D_A)]
        out_shape.append(jax.ShapeDtypeStruct((BS, T, D_A), f32))
        args += [sq, skv, cache_t] + [jnp.asarray(c) for c in consts]
        scratch = [pltpu.VMEM((3 * D_A // LANES, tm, LANES), f32)]
    return pl.pallas_call(
        functools.partial(_inproj_kernel, tm=tm, guests=guests),
        grid=(n // tm,),
        in_specs=in_specs,
        out_specs=out_specs,
        out_shape=out_shape,
        scratch_shapes=scratch,
        compiler_params=pltpu.CompilerParams(dimension_semantics=("arbitrary",), vmem_limit_bytes=VMEM_LIMIT),
        name="inproj",
    )(*args)


def _mlstm_head(q, k, v, a_col, b, a_row, m, cmat, nvec, causal, eye):
    L = q.shape[0]
    qb, kb, vb = _bf(q), _bf(k), _bf(v)
    if a_row is None:
        a_row = jnp.sum(jnp.where(eye, a_col, 0.0), axis=0, keepdims=True)
    dlog = jnp.where(causal, b + a_row, _NEG_INF)
    inter = b + m
    mt = jnp.maximum(inter, jnp.max(dlog, axis=1, keepdims=True))
    s = _dot_nt(qb, kb) * jnp.exp(dlog - mt)
    e_inter = jnp.exp(inter - mt)
    num = _dot(_bf(s), vb) + e_inter * _dot_nt(qb, _bf(cmat))
    den = jnp.sum(s, axis=1, keepdims=True) + e_inter * jnp.sum(q * nvec, axis=1, keepdims=True)
    hh = num / jnp.maximum(jnp.abs(den), jnp.exp(-mt))

    b_last = b[L - 1:L, :]
    wlog = b_last + a_col
    m_new = jnp.maximum(b_last + m, jnp.max(wlog, axis=0, keepdims=True))
    w = jnp.exp(wlog - m_new)
    decay = jnp.exp(b_last + m - m_new)
    wv = w * v
    if L < HD_M:
        pad = jnp.zeros((HD_M - L, HD_M), jnp.float32)
        wv_t = _bf(jnp.concatenate([wv, pad], axis=0).T)
        k_pad = _bf(jnp.concatenate([k, pad], axis=0))
    else:
        wv_t = _bf(wv.T)
        k_pad = kb
    c_new = decay * cmat + _dot(wv_t, k_pad)
    n_new = decay * nvec + jnp.sum(w * k, axis=0, keepdims=True)
    return hh, c_new, n_new, m_new


def _mlstm_kernel(qk_ref, v_ref, o_ref, g_ref, bias_ref, wconv_ref, gmh_ref, conv0_ref, c0_ref, n0_ref, m0_ref,
                  h_ref, c_out, n_out, m_out, conv_out, ubuf, c_s, n_s, m_s, *, chunk, group):
    L = chunk
    c = pl.program_id(1)
    hist = CONV_W - 1

    @pl.when(c == 0)
    def _():
        ubuf[:, _CONV_PAD - hist:_CONV_PAD, :] = conv0_ref[...]
        c_s[...] = c0_ref[0]
        n_s[...] = n0_ref[...]
        m_s[...] = m0_ref[...]

    row_id = lax.broadcasted_iota(jnp.int32, (L, GATE_PAD), 0)
    ti = lax.broadcasted_iota(jnp.int32, (L, L), 0)
    si = lax.broadcasted_iota(jnp.int32, (L, L), 1)
    causal = si <= ti
    eye = si == ti

    for g in range(group):
        ubuf[g, _CONV_PAD:_CONV_PAD + L, :] = qk_ref[g]
        conv = ubuf[g, _CONV_PAD - hist:_CONV_PAD - hist + L, :] * wconv_ref[0:1, :]
        for i in range(1, CONV_W):
            conv = conv + ubuf[g, _CONV_PAD - hist + i:_CONV_PAD - hist + i + L, :] * wconv_ref[i:i + 1, :]
        tail = ubuf[g, _CONV_PAD + L - hist:_CONV_PAD + L, :]
        ubuf[g, _CONV_PAD - hist:_CONV_PAD, :] = tail
        conv_out[g] = tail
        qk = conv * jax.nn.sigmoid(conv)

        gates = g_ref[g] + bias_ref[...]
        lf = jnp.minimum(gates, 0.0) - jnp.log(1.0 + jnp.exp(-jnp.abs(gates)))
        bcum = lf
        step = 1
        while step < L:
            bcum = bcum + jnp.where(row_id >= step, pltpu.roll(bcum, step, 0), 0.0)
            step *= 2

        a_all = gates - pltpu.roll(bcum, GATE_PAD - H_M, 1)
        a_rows = a_all.T if L % LANES == 0 else None

        for h in range(H_M):
            lanes = slice(h * HD_M, (h + 1) * HD_M)
            hh, c_new, n_new, m_new = _mlstm_head(
                qk[:, lanes], qk[:, D_M + h * HD_M:D_M + (h + 1) * HD_M] * (HD_M ** -0.5), v_ref[g, :, lanes],
                a_all[:, h:h + 1], bcum[:, H_M + h:H_M + h + 1], None if a_rows is None else a_rows[h:h + 1, :],
                m_s[g, h][:, 0:1], c_s[g, h], n_s[g, h], causal, eye)
            c_s[g, h] = c_new
            n_s[g, h] = n_new
            m_s[g, h] = jnp.broadcast_to(m_new, (1, LANES))
            h_ref[g, :, lanes] = jax.nn.sigmoid(o_ref[g, :, lanes]) * _rms(hh, gmh_ref[:, lanes])

    @pl.when(c == pl.num_programs(1) - 1)
    def _():
        c_out[...] = c_s[...]
        n_out[...] = n_s[...]
        m_out[...] = m_s[...]


def _mlstm_tiling(B, T):
    chunk = _row_tile(T, MLSTM_MAX_CHUNK)
    group = _row_tile(B, max(1, min(MLSTM_MAX_GROUP, MLSTM_GROUP_TOKENS // chunk)))
    return chunk, group


def _mlstm(qk, vm, om, gates, b_gates, w_conv, g_mh, conv0, c0, n0, m0, layer):
    B, T, _ = qk.shape
    chunk, G = _mlstm_tiling(B, T)
    tok = lambda w: pl.BlockSpec((G, chunk, w), lambda b, c: (b, c, 0))
    per_b = lambda shape: pl.BlockSpec((G,) + shape, lambda b, c: (b,) + (0,) * len(shape))
    hist = CONV_W - 1
    f32 = jnp.float32
    return pl.pallas_call(
        functools.partial(_mlstm_kernel, chunk=chunk, group=G),
        grid=(B // G, T // chunk),
        in_specs=[tok(2 * D_M), tok(D_M), tok(D_M), tok(GATE_PAD),
                  _resident((1, GATE_PAD)), _resident((CONV_W, 2 * D_M)), _resident((1, D_M)),
                  per_b((hist, 2 * D_M)),
                  pl.BlockSpec((1, G, H_M, HD_M, HD_M), lambda b, c: (layer, b, 0, 0, 0)),
                  per_b((H_M, 1, HD_M)), per_b((H_M, 1, LANES))],
        out_specs=[tok(D_M), per_b((H_M, HD_M, HD_M)), per_b((H_M, 1, HD_M)), per_b((H_M, 1, LANES)),
                   per_b((hist, 2 * D_M))],
        out_shape=[jax.ShapeDtypeStruct((B, T, D_M), f32),
                   jax.ShapeDtypeStruct((B, H_M, HD_M, HD_M), f32),
                   jax.ShapeDtypeStruct((B, H_M, 1, HD_M), f32),
                   jax.ShapeDtypeStruct((B, H_M, 1, LANES), f32),
                   jax.ShapeDtypeStruct((B, hist, 2 * D_M), f32)],
        scratch_shapes=[pltpu.VMEM((G, _CONV_PAD + chunk, 2 * D_M), f32),
                        pltpu.VMEM((G, H_M, HD_M, HD_M), f32),
                        pltpu.VMEM((G, H_M, 1, HD_M), f32),
                        pltpu.VMEM((G, H_M, 1, LANES), f32)],
        compiler_params=pltpu.CompilerParams(dimension_semantics=("arbitrary", "arbitrary"),
                                             vmem_limit_bytes=VMEM_LIMIT),
        name="mlstm",
    )(qk, vm, om, gates, b_gates, w_conv, g_mh, conv0, c0, n0, m0)


def _mlstm_short_kernel(qk_ref, v_ref, o_ref, g_ref, hist_ref, bias_ref, wconv_ref, gmh_ref, c0_ref, n0_ref, m0_ref,
                        h_ref, c_out, n_out, m_out, qx, kx, *, group):
    G, T = group, SUBLANES
    R = G * T
    f32 = jnp.float32

    @pl.when(pl.program_id(0) == 0)
    def _():
        qx[...] = jnp.zeros(qx.shape, f32)
        kx[...] = jnp.zeros(kx.shape, f32)

    grp = lambda x: x.reshape(G, T, x.shape[-1])
    rows_of = lambda x: jnp.broadcast_to(x[:, None, :], (G, T, x.shape[-1])).reshape(R, x.shape[-1])
    shift_gate = lambda x: pltpu.roll(x, GATE_PAD - H_M, 1)

    t_wide = lax.broadcasted_iota(jnp.int32, (R, 2 * D_M), 0) % T
    u = qk_ref[...]
    hist = hist_ref[...]
    conv = u * wconv_ref[CONV_W - 1:CONV_W, :]
    for k in range(1, CONV_W):
        back = jnp.where(t_wide >= k, pltpu.roll(u, k, 0), pltpu.roll(hist, (R - (CONV_W - 1 - k)) % R, 0))
        conv = conv + back * wconv_ref[CONV_W - 1 - k:CONV_W - k, :]
    qk = conv * jax.nn.sigmoid(conv)

    gates = g_ref[...] + bias_ref[...]
    lf = jnp.minimum(gates, 0.0) - jnp.log(1.0 + jnp.exp(-jnp.abs(gates)))
    t_id = lax.broadcasted_iota(jnp.int32, (R, GATE_PAD), 0) % T
    bcum = lf
    step = 1
    while step < T:
        bcum = bcum + jnp.where(t_id >= step, pltpu.roll(bcum, step, 0), 0.0)
        step *= 2

    a_all = gates - shift_gate(bcum)
    a_rows = a_all.T
    b_last = shift_gate(rows_of(grp(bcum)[:, T - 1, :]))
    m_rows = rows_of(m0_ref[...])
    wlog = b_last + a_all
    m_new = jnp.maximum(b_last + m_rows, rows_of(jnp.max(grp(wlog), axis=1)))
    w_all = jnp.exp(wlog - m_new)
    decay = grp(jnp.exp(b_last + m_rows - m_new))[:, 0, :]
    m_out[...] = grp(m_new)[:, 0, :]

    ri = lax.broadcasted_iota(jnp.int32, (R, R), 0)
    ci = lax.broadcasted_iota(jnp.int32, (R, R), 1)
    mask = (ri // T == ci // T) & (ci <= ri)

    for h in range(H_M):
        lanes = slice(h * HD_M, (h + 1) * HD_M)
        q = qk[:, lanes]
        k = qk[:, D_M + h * HD_M:D_M + (h + 1) * HD_M] * (HD_M ** -0.5)
        v = v_ref[:, lanes]
        qb, kb, vb = _bf(q), _bf(k), _bf(v)
        b = bcum[:, H_M + h:H_M + h + 1]
        m = m_rows[:, h:h + 1]
        dlog = jnp.where(mask, b + a_rows[h:h + 1, :], _NEG_INF)
        inter = b + m
        mt = jnp.maximum(inter, jnp.max(dlog, axis=1, keepdims=True))
        s = _dot_nt(qb, kb) * jnp.exp(dlog - mt)
        e_inter = jnp.exp(inter - mt)
        for g in range(G):
            qx[g * T:(g + 1) * T, g * HD_M:(g + 1) * HD_M] = q[g * T:(g + 1) * T, :]
            kx[g * T:(g + 1) * T, g * HD_M:(g + 1) * HD_M] = k[g * T:(g + 1) * T, :]
        c_cat = jnp.concatenate([c0_ref[0, g, h] for g in range(G)], axis=1)
        n_h = n0_ref[h]
        num = _dot(_bf(s), vb) + e_inter * _dot_nt(_bf(qx[...]), _bf(c_cat))
        den = jnp.sum(s, axis=1, keepdims=True) + e_inter * jnp.sum(q * rows_of(n_h), axis=1, keepdims=True)
        hh = num / jnp.maximum(jnp.abs(den), jnp.exp(-mt))
        h_ref[:, lanes] = jax.nn.sigmoid(o_ref[:, lanes]) * _rms(hh, gmh_ref[:, lanes])

        w = w_all[:, h:h + 1]
        upd = _dot(_bf((w * v).T), _bf(kx[...]))
        for g in range(G):
            c_out[g, h] = decay[g:g + 1, h:h + 1] * c0_ref[0, g, h] + upd[:, g * HD_M:(g + 1) * HD_M]
        n_out[h] = decay[:, h:h + 1] * n_h + jnp.sum(grp(w * k), axis=1)


def _mlstm_short(qk, vm, om, gates, b_gates, w_conv, g_mh, hist, c0, n0, m0, layer, B):
    T = SUBLANES
    G = LANES // T
    assert B % G == 0 and qk.shape[0] == B * T
    R = G * T
    row = lambda w: pl.BlockSpec((R, w), lambda i: (i, 0))
    f32 = jnp.float32
    return pl.pallas_call(
        functools.partial(_mlstm_short_kernel, group=G),
        grid=(B // G,),
        in_specs=[row(2 * D_M), row(D_M), row(D_M), row(GATE_PAD), row(2 * D_M),
                  _resident((1, GATE_PAD)), _resident((CONV_W, 2 * D_M)), _resident((1, D_M)),
                  pl.BlockSpec((1, G, H_M, HD_M, HD_M), lambda i: (layer, i, 0, 0, 0)),
                  pl.BlockSpec((H_M, G, HD_M), lambda i: (0, i, 0)),
                  pl.BlockSpec((G, GATE_PAD), lambda i: (i, 0))],
        out_specs=[row(D_M), pl.BlockSpec((G, H_M, HD_M, HD_M), lambda i: (i, 0, 0, 0)),
                   pl.BlockSpec((H_M, G, HD_M), lambda i: (0, i, 0)), pl.BlockSpec((G, GATE_PAD), lambda i: (i, 0))],
        out_shape=[jax.ShapeDtypeStruct((B * T, D_M), f32), jax.ShapeDtypeStruct((B, H_M, HD_M, HD_M), f32),
                   jax.ShapeDtypeStruct((H_M, B, HD_M), f32), jax.ShapeDtypeStruct((B, GATE_PAD), f32)],
        scratch_shapes=[pltpu.VMEM((R, G * HD_M), f32), pltpu.VMEM((R, G * HD_M), f32)],
        compiler_params=pltpu.CompilerParams(dimension_semantics=("arbitrary",), vmem_limit_bytes=VMEM_LIMIT),
        name="mlstm_short",
    )(qk, vm, om, gates, hist, b_gates, w_conv, g_mh, c0, n0, m0)


def _pattn_kernel(q_ref, kv_ref, o_ref, kvt_ref, prev_s, acc_s, m_s, l_s, s_buf, p_buf, m_buf, tok_s):
    i = pl.program_id(1)
    nb = ATT_BLOCK
    pair = 2 * HD_A
    scale = HD_A ** -0.5

    @pl.when(i == 0)
    def _():
        prev_s[...] = jnp.zeros(prev_s.shape, prev_s.dtype)

    pq = lax.broadcasted_iota(jnp.int32, (nb, nb), 0)
    pk = lax.broadcasted_iota(jnp.int32, (nb, nb), 1)
    lo_half = lax.broadcasted_iota(jnp.int32, (nb, pair), 1) < HD_A
    ones = jnp.ones((2 * nb, pair), jnp.bfloat16)
    no_keys = jnp.full((nb, nb), _NEG_INF, jnp.float32)
    npat = len(DILATED_PATTERNS)
    npair = H_A // 2

    for pi, (win, dil) in enumerate(DILATED_PATTERNS):
        first, last = pi == 0, pi == npat - 1
        nrun = ATT_STRIDE // dil
        rlen = nb // nrun
        nblk = nrun
        dist = nrun * (pq % rlen) + pq // rlen - (nrun * (pk % rlen) + pk // rlen)
        nd_cur = jnp.where(dist >= 0, -(dist * dil).astype(jnp.float32), _NEG_INF)
        nd_prev = jnp.where(dist <= 0, -((dist + nb) * dil).astype(jnp.float32), _NEG_INF)
        nd_prev_first = jnp.where(i > 0, nd_prev, no_keys)

        def starts(c, jb, dil=dil, nrun=nrun, rlen=rlen):
            return [pl.multiple_of((dil * e + c) * nb + rlen * jb, SUBLANES) for e in range(nrun)]

        def load(ref, j, st, rlen=rlen):
            return jnp.concatenate([ref[j, pl.ds(s, rlen), :] for s in st], axis=0)

        def store(ref, j, st, val, rlen=rlen):
            for e, s in enumerate(st):
                ref[j, pl.ds(s, rlen), :] = val[e * rlen:(e + 1) * rlen]

        def unit(c, jb, prev_ref, prev_jb, nd_prev_u, first=first, last=last, nd_cur=nd_cur, starts=starts,
                 load=load, store=store):
            rows = starts(c, jb)
            prows = starts(c, prev_jb)
            nd = jnp.concatenate([nd_cur, nd_prev_u], axis=1)
            for hp in range(npair):
                q2 = load(q_ref, hp, rows) * scale
                k2 = _bf(jnp.concatenate([load(kv_ref, hp, rows), load(prev_ref, hp, prows)], axis=0))
                for e in range(2):
                    qm = _bf(jnp.where(lo_half if e == 0 else jnp.logical_not(lo_half), q2, 0.0))
                    s_buf[2 * hp + e] = _dot_nt(qm, k2) + _SLOPES[2 * hp + e] * nd
            for h in range(H_A):
                s = s_buf[h]
                m = jnp.max(s, axis=1, keepdims=True)
                p_buf[h] = _bf(jnp.exp(s - m))
                m_buf[h] = jnp.broadcast_to(m, (nb, pair))
            for hp in range(npair):
                v2 = _bf(jnp.concatenate([load(kv_ref, npair + hp, rows), load(prev_ref, npair + hp, prows)], axis=0))
                rhs = jnp.concatenate([v2, ones], axis=1)
                r_e = _dot(p_buf[2 * hp], rhs)
                r_o = _dot(p_buf[2 * hp + 1], rhs)
                acc2 = jnp.where(lo_half, r_e[:, :pair], r_o[:, :pair])
                l2 = jnp.where(lo_half, r_e[:, pair:], r_o[:, pair:])
                m2 = jnp.where(lo_half, m_buf[2 * hp], m_buf[2 * hp + 1])
                if not first:
                    m_old = load(m_s, hp, rows)
                    m_new = jnp.maximum(m_old, m2)
                    a_old = jnp.exp(m_old - m_new)
                    a_loc = jnp.exp(m2 - m_new)
                    acc2 = a_old * load(acc_s, hp, rows) + a_loc * acc2
                    l2 = a_old * load(l_s, hp, rows) + a_loc * l2
                    m2 = m_new
                if last:
                    o_ref[hp, pl.ds(c, nb, stride=ATT_STRIDE), :] = acc2 / l2
                else:
                    store(acc_s, hp, rows, acc2)
                    store(m_s, hp, rows, m2)
                    store(l_s, hp, rows, l2)

        def first_block(c, carry, nblk=nblk, unit=unit, nd_prev_first=nd_prev_first):
            unit(c, 0, prev_s, nblk - 1, nd_prev_first)
            return carry

        def later_block(u, carry, dil=dil, unit=unit, nd_prev=nd_prev):
            jb = 1 + u // dil
            unit(u % dil, jb, kv_ref, jb - 1, nd_prev)
            return carry

        lax.fori_loop(0, dil, first_block, 0)
        if nblk > 1:
            lax.fori_loop(0, dil * (nblk - 1), later_block, 0)

    prev_s[...] = kv_ref[...]

    @pl.when(i == pl.num_programs(1) - 1)
    def _():
        for j in range(2 * npair):
            for r in range(ATT_STRIDE):
                tok_s[pl.ds(r, nb, stride=ATT_STRIDE), :] = kv_ref[j, r * nb:(r + 1) * nb, :]
            t = tok_s[...].T
            kvt_ref[0, j // npair, 2 * (j % npair)] = t[0:HD_A]
            kvt_ref[0, j // npair, 2 * (j % npair) + 1] = t[HD_A:pair]


def _prompt_attention(q, kv, B):
    nq, n, _ = q.shape
    S = n // B
    for win, dil in DILATED_PATTERNS:
        assert win // dil == ATT_BLOCK and ATT_TILE % (dil * ATT_BLOCK) == 0
    assert S % ATT_TILE == 0 and LANES == 2 * HD_A
    assert DILATED_PATTERNS[-1][1] == ATT_STRIDE and all(ATT_STRIDE % d == 0 for _, d in DILATED_PATTERNS)
    nt = S // ATT_TILE
    tile = lambda k: pl.BlockSpec((k, ATT_TILE, LANES), lambda b, i: (0, b * nt + i, 0))
    f32 = jnp.float32
    return pl.pallas_call(
        _pattn_kernel,
        grid=(B, nt),
        in_specs=[tile(nq), tile(2 * nq)],
        out_specs=[tile(nq), pl.BlockSpec((1, 2, H_A, HD_A, ATT_TILE), lambda b, i: (b, 0, 0, 0, 0))],
        out_shape=[jax.ShapeDtypeStruct((nq, n, LANES), f32),
                   jax.ShapeDtypeStruct((B, 2, H_A, HD_A, ATT_TILE), f32)],
        scratch_shapes=[pltpu.VMEM((2 * nq, ATT_TILE, LANES), f32), pltpu.VMEM((nq, ATT_TILE, LANES), f32),
                        pltpu.VMEM((nq, ATT_TILE, LANES), f32), pltpu.VMEM((nq, ATT_TILE, LANES), f32),
                        pltpu.VMEM((H_A, ATT_BLOCK, 2 * ATT_BLOCK), f32),
                        pltpu.VMEM((H_A, ATT_BLOCK, 2 * ATT_BLOCK), jnp.bfloat16),
                        pltpu.VMEM((H_A, ATT_BLOCK, LANES), f32), pltpu.VMEM((ATT_TILE, LANES), f32)],
        compiler_params=pltpu.CompilerParams(dimension_semantics=("arbitrary", "arbitrary"),
                                             vmem_limit_bytes=VMEM_LIMIT),
        name="pattn",
    )(q, kv)


def _sample_tables(wb, t_new):
    rows = np.arange(H_A * t_new)
    t = rows % t_new
    slope = np.asarray(_SLOPES)[rows // t_new]
    pos = np.arange(wb + t_new)
    d = wb + t[:, None] - pos[None, :]
    cnt = np.zeros(d.shape, np.float32)
    for win, dil in DILATED_PATTERNS:
        cnt += ((d >= 0) & (d % dil == 0) & (d <= win)).astype(np.float32)
    bias = np.where(cnt > 0, -slope[:, None] * d, -np.inf).astype(np.float32)
    return bias, cnt


def _outffn_kernel(x_ref, hm_ref, ha_ref, wo_ref, g1_ref, g2_ref, g3_ref, wg_ref, wu_ref, wd_ref, y_ref):
    ha = jnp.concatenate([ha_ref[j] for j in range(D_A // LANES)], axis=1)
    mix = _dot(_bf(hm_ref[...]), wo_ref[0:D_M, :]) + _dot(_bf(ha), wo_ref[D_M:D_M + D_A, :])
    x1 = x_ref[...] + _rms(mix, g1_ref[...])
    hf = _bf(_rms(x1, g2_ref[...]))
    gate = _dot(hf, wg_ref[...])
    up = _dot(hf, wu_ref[...])
    f = _dot(_bf(gate * jax.nn.sigmoid(gate) * up), wd_ref[...])
    y_ref[...] = x1 + _rms(f, g3_ref[...])


def _outffn(x, hm, ha, w_out, g_mix_post, g_ffn_pre, g_ffn_post, w_gate, w_up, w_down, tm):
    n, d = x.shape
    row = lambda w: pl.BlockSpec((tm, w), lambda i: (i, 0))
    return pl.pallas_call(
        _outffn_kernel,
        grid=(n // tm,),
        in_specs=[row(d), row(D_M), pl.BlockSpec((D_A // LANES, tm, LANES), lambda i: (0, i, 0)),
                  _resident(w_out.shape), _resident((1, d)), _resident((1, d)),
                  _resident((1, d)), _resident(w_gate.shape), _resident(w_up.shape), _resident(w_down.shape)],
        out_specs=row(d),
        out_shape=jax.ShapeDtypeStruct((n, d), jnp.float32),
        compiler_params=pltpu.CompilerParams(dimension_semantics=("arbitrary",), vmem_limit_bytes=VMEM_LIMIT),
        name="outffn",
    )(x, hm, ha, w_out, g_mix_post, g_ffn_pre, g_ffn_post, w_gate, w_up, w_down)


def _layer(xp, xs, lw, conv_s, c_all, n_s, m_s, cache, layer):
    B, S, D = xp.shape
    BS, T, _ = xs.shape
    assert T == SUBLANES and BS % (LANES // SUBLANES) == 0 and MAX_WINDOW == ATT_TILE
    f32 = jnp.float32
    hist = CONV_W - 1
    n_p, n_sm = B * S, BS * T
    xpf, xsf = xp.reshape(n_p, D), xs.reshape(n_sm, D)
    rows = lambda a, b, t: jnp.transpose(a, (1, 0, 2)).reshape(b, t, a.shape[0] * LANES)
    ffn = lambda x, hm, ha: _outffn(x, hm, ha, lw["w_out"], lw["g_mix_post"], lw["g_ffn_pre"], lw["g_ffn_post"],
                                    lw["w_gate"], lw["w_up"], lw["w_down"], _row_tile(x.shape[0], 512))

    qk_s, vm_s, om_s, qa_s, kva_s, gates_s = _inproj(xsf, lw["g_mix_pre"], lw["w_main"], lw["w_gates"],
                                                     _row_tile(n_sm, 1024))
    kv_rows = rows(kva_s, BS, T)
    qk, vm, om, qa, kva, gates, ha_s = _inproj(xpf, lw["g_mix_pre"], lw["w_main"], lw["w_gates"],
                                               _row_tile(S, INPROJ_LONG_ROWS),
                                               guest=(rows(qa_s, BS, T), kv_rows, cache, layer))

    r3 = lambda a: a.reshape(B, S, a.shape[-1])
    hm, c_p, n_p_new, m_p, conv_p = _mlstm(
        r3(qk), r3(vm), r3(om), r3(gates), lw["b_gates"], lw["w_conv"], lw["g_mh"],
        jnp.zeros((B, hist, 2 * D_M), f32), jnp.zeros((1, B, H_M, HD_M, HD_M), f32),
        jnp.zeros((B, H_M, 1, HD_M), f32), jnp.zeros((B, H_M, 1, LANES), f32), 0)
    ha, kv_t = _prompt_attention(qa, kva, B)
    yp = ffn(xpf, hm.reshape(n_p, D_M), ha)
    out_p = (jnp.transpose(kv_t, (0, 4, 1, 2, 3)),
             c_p, n_p_new.reshape(B, H_M, HD_M), m_p[:, :, 0, 0], conv_p)

    conv_rows = jnp.pad(conv_s, ((0, 0), (0, T - hist), (0, 0))).reshape(n_sm, 2 * D_M)
    hm_s, c_s, n_s_new, m_s_new = _mlstm_short(
        qk_s, vm_s, om_s, gates_s, lw["b_gates"], lw["w_conv"], lw["g_mh"], conv_rows, c_all,
        jnp.transpose(n_s, (1, 0, 2)), jnp.pad(m_s, ((0, 0), (0, GATE_PAD - H_M))), layer, BS)
    ys = ffn(xsf, hm_s, jnp.transpose(ha_s.reshape(n_sm, D_A // LANES, LANES), (1, 0, 2)))
    out_s = (kv_rows.reshape(BS, T, 2, H_A, HD_A), c_s, jnp.transpose(n_s_new, (1, 0, 2)), m_s_new[:, :H_M],
             qk_s.reshape(BS, T, 2 * D_M)[:, T - hist:])
    return yp.reshape(B, S, D), ys.reshape(BS, T, D), out_p, out_s


def kernel(x_prompt, x_sample, cache_kv, state_C, state_n, state_m, state_conv, g_mix_pre, g_mix_post, g_ffn_pre,
           g_ffn_post, w_in, b_gates, w_conv, g_mh, w_out, w_gate, w_up, w_down):
    depth = w_in.shape[0]
    g0 = 4 * D_M
    g1 = g0 + 2 * H_M
    cache = jnp.transpose(cache_kv, (0, 1, 3, 4, 5, 2))
    w_in_t = _bf(jnp.transpose(w_in, (0, 2, 1)))

    hp, hs = x_prompt, x_sample
    outs_p, outs_s = [], []
    for l in range(depth):
        lw = {
            "g_mix_pre": g_mix_pre[l][None], "g_mix_post": g_mix_post[l][None],
            "g_ffn_pre": g_ffn_pre[l][None], "g_ffn_post": g_ffn_post[l][None],
            "w_main": jnp.concatenate([w_in_t[l, :g0], w_in_t[l, g1:]], axis=0),
            "w_gates": jnp.pad(w_in_t[l, g0:g1], ((0, GATE_PAD - 2 * H_M), (0, 0))),
            "b_gates": jnp.pad(b_gates[l], (0, GATE_PAD - 2 * H_M))[None],
            "w_conv": w_conv[l], "g_mh": g_mh[l][None],
            "w_out": _bf(w_out[l]), "w_gate": _bf(w_gate[l]), "w_up": _bf(w_up[l]), "w_down": _bf(w_down[l]),
        }
        hp, hs, out_p, out_s = _layer(hp, hs, lw, state_conv[l], state_C, state_n[l], state_m[l], cache, l)
        outs_p.append(out_p)
        outs_s.append(out_s)
    stack = lambda outs, i: jnp.stack([o[i] for o in outs])
    return (hp, hs, stack(outs_p, 0), stack(outs_s, 0),
            stack(outs_p, 1), stack(outs_p, 2), stack(outs_p, 3), stack(outs_p, 4),
            stack(outs_s, 1), stack(outs_s, 2), stack(outs_s, 3), stack(outs_s, 4))
```

```python
import functools

import numpy as np
import jax
import jax.numpy as jnp
from jax import lax
from jax.experimental import pallas as pl
from jax.experimental.pallas import tpu as pltpu

H_M = 5
HD_M = 128
D_M = H_M * HD_M
H_A = 6
HD_A = 64
D_A = H_A * HD_A
CONV_W = 4
MLSTM_MAX_CHUNK = 256
MLSTM_GROUP_TOKENS = 256
MLSTM_MAX_GROUP = 4
DILATED_PATTERNS = ((128, 1), (512, 4), (2048, 16))
MAX_WINDOW = 2048
EPS = 1e-6
LANES = 128
GATE_PAD = LANES
HOST_ROWS = 256
SUBLANES = 8
ATT_BLOCK = 128
ATT_STRIDE = max(d for _, d in DILATED_PATTERNS)
ATT_TILE = ATT_BLOCK * ATT_STRIDE
VMEM_LIMIT = 56 * 1024 * 1024

_SLOPES = [2.0 ** (-8.0 * (h + 1) / H_A) for h in range(H_A)]
_NEG_INF = float("-inf")


def _bf(x):
    return x.astype(jnp.bfloat16)


def _dot(a, b):
    return jnp.dot(a, b, preferred_element_type=jnp.float32)


def _dot_nt(a, b):
    return lax.dot_general(a, b, (((1,), (1,)), ((), ())), preferred_element_type=jnp.float32)


def _rms(x, g):
    return x * lax.rsqrt(jnp.mean(x * x, axis=-1, keepdims=True) + EPS) * g


def _resident(shape):
    nd = len(shape)
    return pl.BlockSpec(shape, lambda *_: (0,) * nd, pipeline_mode=pl.Buffered(1))


def _row_tile(n, cap):
    t = min(n, cap)
    while n % t:
        t //= 2
    return t


_IN_CUTS = (0, 2 * D_M, 3 * D_M, 4 * D_M, 4 * D_M + D_A, 4 * D_M + 3 * D_A)


_CONV_PAD = 8


def _sample_attention_rows(q_ref, kvn_ref, cache_ref, bias_ref, cnt_ref, biasn_ref, cntn_ref, o_ref, b):
    t_new = q_ref.shape[1]
    scale = HD_A ** -0.5
    heads = [(h * HD_A, (h + 1) * HD_A) for h in range(H_A)]
    qh = [_bf(q_ref[b, :, lo:hi]) for lo, hi in heads]
    sc = jnp.concatenate([_dot(qh[h], _bf(cache_ref[0, b, 0, h])) for h in range(H_A)], axis=0)
    sn = jnp.concatenate([_dot_nt(qh[h], _bf(kvn_ref[b, :, lo:hi])) for h, (lo, hi) in enumerate(heads)], axis=0)
    sc = sc * scale + bias_ref[...]
    sn = sn * scale + biasn_ref[...]
    mx = jnp.maximum(jnp.max(sc, axis=1, keepdims=True), jnp.max(sn, axis=1, keepdims=True))
    pc = cnt_ref[...] * jnp.exp(sc - mx)
    pn = cntn_ref[...] * jnp.exp(sn - mx)
    den = jnp.sum(pc, axis=1, keepdims=True) + jnp.sum(pn, axis=1, keepdims=True)
    outs = []
    for h, (lo, hi) in enumerate(heads):
        r0, r1 = h * t_new, (h + 1) * t_new
        o = (_dot_nt(_bf(pc[r0:r1]), _bf(cache_ref[0, b, 1, h]))
             + _dot(_bf(pn[r0:r1]), _bf(kvn_ref[b, :, D_A + lo:D_A + hi])))
        outs.append(o / den[r0:r1])
    o_ref[b] = jnp.concatenate(outs, axis=1)


def _inproj_kernel(*refs, tm, guests):
    if not guests:
        x_ref, g_ref, w_ref, wg_ref, qk_ref, vm_ref, om_ref, qa_ref, kva_ref, gates_ref = refs
    else:
        (x_ref, g_ref, w_ref, wg_ref, sq_ref, skv_ref, cache_ref, bias_ref, cnt_ref, biasn_ref, cntn_ref,
         qk_ref, vm_ref, om_ref, qa_ref, kva_ref, gates_ref, sha_ref, zs) = refs
    a = _bf(_rms(x_ref[...], g_ref[...]))
    nq = D_A // LANES
    att_tile = lambda j: (qa_ref, j) if j < nq else (kva_ref, j - nq)
    z = _dot_nt(a, w_ref[_IN_CUTS[3]:_IN_CUTS[5], :])
    if not guests:
        for j in range(3 * nq):
            dst, jj = att_tile(j)
            dst[jj] = z[:, j * LANES:(j + 1) * LANES]
    else:
        for j in range(3 * nq):
            zs[j] = z[:, j * LANES:(j + 1) * LANES]
    qk_ref[...] = _dot_nt(a, w_ref[_IN_CUTS[0]:_IN_CUTS[1], :])
    if guests:
        sub = pl.program_id(0) % (ATT_TILE // tm)
        per = tm // ATT_STRIDE
        for j in range(3 * nq):
            dst, jj = att_tile(j)
            for r in range(ATT_STRIDE):
                start = pl.multiple_of(r * ATT_BLOCK + sub * per, SUBLANES)
                dst[jj, pl.ds(start, per), :] = zs[j, pl.ds(r, per, stride=ATT_STRIDE), :]
    vm_ref[...] = _dot_nt(a, w_ref[_IN_CUTS[1]:_IN_CUTS[2], :])
    for b in range(guests):
        _sample_attention_rows(sq_ref, skv_ref, cache_ref, bias_ref, cnt_ref, biasn_ref, cntn_ref, sha_ref, b)
    om_ref[...] = _dot_nt(a, w_ref[_IN_CUTS[2]:_IN_CUTS[3], :])
    gates_ref[...] = _dot_nt(a, wg_ref[...])


def _guest_plumbing(guest, steps):
    sq, skv, cache_t, layer, first, count = guest
    T = sq.shape[1]
    wb = cache_t.shape[-1]
    assert count % steps == 0 and first % (count // steps) == 0
    per_step = count // steps
    off = first // per_step
    bias, cnt = _sample_tables(wb, T)
    consts = [bias[:, :wb], cnt[:, :wb], bias[:, wb:], cnt[:, wb:]]
    per_g = lambda w: pl.BlockSpec((per_step, T, w), lambda i: (off + i, 0, 0))
    in_specs = [per_g(D_A), per_g(2 * D_A),
                pl.BlockSpec((1, per_step, 2, H_A, HD_A, wb), lambda i: (layer, off + i, 0, 0, 0, 0))]
    in_specs += [_resident(c.shape) for c in consts]
    args = [sq, skv, cache_t] + [jnp.asarray(c) for c in consts]
    out_spec = pl.BlockSpec((per_step, T, D_A), lambda i: (i, 0, 0))
    return per_step, in_specs, args, out_spec, jax.ShapeDtypeStruct((count, T, D_A), jnp.float32)


def _inproj(x, g, w_main, w_gates, tm, guest=None):
    n, d = x.shape
    row = lambda w: pl.BlockSpec((tm, w), lambda i: (i, 0))
    f32 = jnp.float32
    in_specs = [row(d), _resident((1, d)), _resident(w_main.shape), _resident(w_gates.shape)]
    out_shape = [jax.ShapeDtypeStruct((n, 2 * D_M), f32), jax.ShapeDtypeStruct((n, D_M), f32),
                 jax.ShapeDtypeStruct((n, D_M), f32), jax.ShapeDtypeStruct((D_A // LANES, n, LANES), f32),
                 jax.ShapeDtypeStruct((2 * D_A // LANES, n, LANES), f32), jax.ShapeDtypeStruct((n, GATE_PAD), f32)]
    args = [x, g, w_main, w_gates]
    scratch = []
    guests = 0
    if guest is None:
        tiles = lambda w: pl.BlockSpec((w // LANES, tm, LANES), lambda i: (0, i, 0))
        out_specs = [row(2 * D_M), row(D_M), row(D_M), tiles(D_A), tiles(2 * D_A), row(GATE_PAD)]
    else:
        assert n % ATT_TILE == 0 and ATT_TILE % tm == 0 and tm % (ATT_STRIDE * SUBLANES) == 0
        guests, g_specs, g_args, g_out_spec, g_out_shape = _guest_plumbing(guest, n // tm)
        per_att = ATT_TILE // tm
        tiles = lambda w: pl.BlockSpec((w // LANES, ATT_TILE, LANES), lambda i: (0, i // per_att, 0),
                                       pipeline_mode=pl.Buffered(1))
        in_specs += g_specs
        out_specs = [row(2 * D_M), row(D_M), row(D_M), tiles(D_A), tiles(2 * D_A), row(GATE_PAD), g_out_spec]
        out_shape.append(g_out_shape)
        args += g_args
        scratch = [pltpu.VMEM((3 * D_A // LANES, tm, LANES), f32)]
    return pl.pallas_call(
        functools.partial(_inproj_kernel, tm=tm, guests=guests),
        grid=(n // tm,),
        in_specs=in_specs,
        out_specs=out_specs,
        out_shape=out_shape,
        scratch_shapes=scratch,
        compiler_params=pltpu.CompilerParams(dimension_semantics=("arbitrary",), vmem_limit_bytes=VMEM_LIMIT),
        name="inproj",
    )(*args)


def _mlstm_head(q, k, v, a_col, b, a_row, m, cmat, nvec, causal, eye):
    L = q.shape[0]
    qb, kb, vb = _bf(q), _bf(k), _bf(v)
    if a_row is None:
        a_row = jnp.sum(jnp.where(eye, a_col, 0.0), axis=0, keepdims=True)
    dlog = jnp.where(causal, b + a_row, _NEG_INF)
    inter = b + m
    mt = jnp.maximum(inter, jnp.max(dlog, axis=1, keepdims=True))
    s = _dot_nt(qb, kb) * jnp.exp(dlog - mt)
    e_inter = jnp.exp(inter - mt)
    num = _dot(_bf(s), vb) + e_inter * _dot_nt(qb, _bf(cmat))
    den = jnp.sum(s, axis=1, keepdims=True) + e_inter * jnp.sum(q * nvec, axis=1, keepdims=True)
    hh = num / jnp.maximum(jnp.abs(den), jnp.exp(-mt))

    b_last = b[L - 1:L, :]
    wlog = b_last + a_col
    m_new = jnp.maximum(b_last + m, jnp.max(wlog, axis=0, keepdims=True))
    w = jnp.exp(wlog - m_new)
    decay = jnp.exp(b_last + m - m_new)
    wv = w * v
    if L < HD_M:
        pad = jnp.zeros((HD_M - L, HD_M), jnp.float32)
        wv_t = _bf(jnp.concatenate([wv, pad], axis=0).T)
        k_pad = _bf(jnp.concatenate([k, pad], axis=0))
    else:
        wv_t = _bf(wv.T)
        k_pad = kb
    c_new = decay * cmat + _dot(wv_t, k_pad)
    n_new = decay * nvec + jnp.sum(w * k, axis=0, keepdims=True)
    return hh, c_new, n_new, m_new


def _mlstm_kernel(qk_ref, v_ref, o_ref, g_ref, bias_ref, wconv_ref, gmh_ref, conv0_ref, c0_ref, n0_ref, m0_ref,
                  h_ref, c_out, n_out, m_out, conv_out, ubuf, c_s, n_s, m_s, *, chunk, group):
    L = chunk
    c = pl.program_id(1)
    hist = CONV_W - 1

    @pl.when(c == 0)
    def _():
        ubuf[:, _CONV_PAD - hist:_CONV_PAD, :] = conv0_ref[...]
        c_s[...] = c0_ref[0]
        n_s[...] = n0_ref[...]
        m_s[...] = m0_ref[...]

    row_id = lax.broadcasted_iota(jnp.int32, (L, GATE_PAD), 0)
    ti = lax.broadcasted_iota(jnp.int32, (L, L), 0)
    si = lax.broadcasted_iota(jnp.int32, (L, L), 1)
    causal = si <= ti
    eye = si == ti

    for g in range(group):
        ubuf[g, _CONV_PAD:_CONV_PAD + L, :] = qk_ref[g]
        conv = ubuf[g, _CONV_PAD - hist:_CONV_PAD - hist + L, :] * wconv_ref[0:1, :]
        for i in range(1, CONV_W):
            conv = conv + ubuf[g, _CONV_PAD - hist + i:_CONV_PAD - hist + i + L, :] * wconv_ref[i:i + 1, :]
        tail = ubuf[g, _CONV_PAD + L - hist:_CONV_PAD + L, :]
        ubuf[g, _CONV_PAD - hist:_CONV_PAD, :] = tail
        conv_out[g] = tail
        qk = conv * jax.nn.sigmoid(conv)

        gates = g_ref[g] + bias_ref[...]
        lf = jnp.minimum(gates, 0.0) - jnp.log(1.0 + jnp.exp(-jnp.abs(gates)))
        bcum = lf
        step = 1
        while step < L:
            bcum = bcum + jnp.where(row_id >= step, pltpu.roll(bcum, step, 0), 0.0)
            step *= 2

        a_all = gates - pltpu.roll(bcum, GATE_PAD - H_M, 1)
        a_rows = a_all.T if L % LANES == 0 else None

        for h in range(H_M):
            lanes = slice(h * HD_M, (h + 1) * HD_M)
            hh, c_new, n_new, m_new = _mlstm_head(
                qk[:, lanes], qk[:, D_M + h * HD_M:D_M + (h + 1) * HD_M] * (HD_M ** -0.5), v_ref[g, :, lanes],
                a_all[:, h:h + 1], bcum[:, H_M + h:H_M + h + 1], None if a_rows is None else a_rows[h:h + 1, :],
                m_s[g, h][:, 0:1], c_s[g, h], n_s[g, h], causal, eye)
            c_s[g, h] = c_new
            n_s[g, h] = n_new
            m_s[g, h] = jnp.broadcast_to(m_new, (1, LANES))
            h_ref[g, :, lanes] = jax.nn.sigmoid(o_ref[g, :, lanes]) * _rms(hh, gmh_ref[:, lanes])

    @pl.when(c == pl.num_programs(1) - 1)
    def _():
        c_out[...] = c_s[...]
        n_out[...] = n_s[...]
        m_out[...] = m_s[...]


def _mlstm_tiling(B, T):
    chunk = _row_tile(T, MLSTM_MAX_CHUNK)
    group = _row_tile(B, max(1, min(MLSTM_MAX_GROUP, MLSTM_GROUP_TOKENS // chunk)))
    return chunk, group


def _mlstm(qk, vm, om, gates, b_gates, w_conv, g_mh, conv0, c0, n0, m0, layer):
    B, T, _ = qk.shape
    chunk, G = _mlstm_tiling(B, T)
    tok = lambda w: pl.BlockSpec((G, chunk, w), lambda b, c: (b, c, 0))
    per_b = lambda shape: pl.BlockSpec((G,) + shape, lambda b, c: (b,) + (0,) * len(shape))
    hist = CONV_W - 1
    f32 = jnp.float32
    return pl.pallas_call(
        functools.partial(_mlstm_kernel, chunk=chunk, group=G),
        grid=(B // G, T // chunk),
        in_specs=[tok(2 * D_M), tok(D_M), tok(D_M), tok(GATE_PAD),
                  _resident((1, GATE_PAD)), _resident((CONV_W, 2 * D_M)), _resident((1, D_M)),
                  per_b((hist, 2 * D_M)),
                  pl.BlockSpec((1, G, H_M, HD_M, HD_M), lambda b, c: (layer, b, 0, 0, 0)),
                  per_b((H_M, 1, HD_M)), per_b((H_M, 1, LANES))],
        out_specs=[tok(D_M), per_b((H_M, HD_M, HD_M)), per_b((H_M, 1, HD_M)), per_b((H_M, 1, LANES)),
                   per_b((hist, 2 * D_M))],
        out_shape=[jax.ShapeDtypeStruct((B, T, D_M), f32),
                   jax.ShapeDtypeStruct((B, H_M, HD_M, HD_M), f32),
                   jax.ShapeDtypeStruct((B, H_M, 1, HD_M), f32),
                   jax.ShapeDtypeStruct((B, H_M, 1, LANES), f32),
                   jax.ShapeDtypeStruct((B, hist, 2 * D_M), f32)],
        scratch_shapes=[pltpu.VMEM((G, _CONV_PAD + chunk, 2 * D_M), f32),
                        pltpu.VMEM((G, H_M, HD_M, HD_M), f32),
                        pltpu.VMEM((G, H_M, 1, HD_M), f32),
                        pltpu.VMEM((G, H_M, 1, LANES), f32)],
        compiler_params=pltpu.CompilerParams(dimension_semantics=("arbitrary", "arbitrary"),
                                             vmem_limit_bytes=VMEM_LIMIT),
        name="mlstm",
    )(qk, vm, om, gates, b_gates, w_conv, g_mh, conv0, c0, n0, m0)


def _mlstm_short_kernel(qk_ref, v_ref, o_ref, g_ref, hist_ref, bias_ref, wconv_ref, gmh_ref, c0_ref, n0_ref, m0_ref,
                        h_ref, c_out, n_out, m_out, qx, kx, *, group):
    G, T = group, SUBLANES
    R = G * T
    f32 = jnp.float32

    @pl.when(pl.program_id(0) == 0)
    def _():
        qx[...] = jnp.zeros(qx.shape, f32)
        kx[...] = jnp.zeros(kx.shape, f32)

    grp = lambda x: x.reshape(G, T, x.shape[-1])
    rows_of = lambda x: jnp.broadcast_to(x[:, None, :], (G, T, x.shape[-1])).reshape(R, x.shape[-1])
    shift_gate = lambda x: pltpu.roll(x, GATE_PAD - H_M, 1)

    t_wide = lax.broadcasted_iota(jnp.int32, (R, 2 * D_M), 0) % T
    u = qk_ref[...]
    hist = hist_ref[...]
    conv = u * wconv_ref[CONV_W - 1:CONV_W, :]
    for k in range(1, CONV_W):
        back = jnp.where(t_wide >= k, pltpu.roll(u, k, 0), pltpu.roll(hist, (R - (CONV_W - 1 - k)) % R, 0))
        conv = conv + back * wconv_ref[CONV_W - 1 - k:CONV_W - k, :]
    qk = conv * jax.nn.sigmoid(conv)

    gates = g_ref[...] + bias_ref[...]
    lf = jnp.minimum(gates, 0.0) - jnp.log(1.0 + jnp.exp(-jnp.abs(gates)))
    t_id = lax.broadcasted_iota(jnp.int32, (R, GATE_PAD), 0) % T
    bcum = lf
    step = 1
    while step < T:
        bcum = bcum + jnp.where(t_id >= step, pltpu.roll(bcum, step, 0), 0.0)
        step *= 2

    a_all = gates - shift_gate(bcum)
    a_rows = a_all.T
    b_last = shift_gate(rows_of(grp(bcum)[:, T - 1, :]))
    m_rows = rows_of(m0_ref[...])
    wlog = b_last + a_all
    m_new = jnp.maximum(b_last + m_rows, rows_of(jnp.max(grp(wlog), axis=1)))
    w_all = jnp.exp(wlog - m_new)
    decay = grp(jnp.exp(b_last + m_rows - m_new))[:, 0, :]
    m_out[...] = grp(m_new)[:, 0, :]

    ri = lax.broadcasted_iota(jnp.int32, (R, R), 0)
    ci = lax.broadcasted_iota(jnp.int32, (R, R), 1)
    mask = (ri // T == ci // T) & (ci <= ri)

    for h in range(H_M):
        lanes = slice(h * HD_M, (h + 1) * HD_M)
        q = qk[:, lanes]
        k = qk[:, D_M + h * HD_M:D_M + (h + 1) * HD_M] * (HD_M ** -0.5)
        v = v_ref[:, lanes]
        qb, kb, vb = _bf(q), _bf(k), _bf(v)
        b = bcum[:, H_M + h:H_M + h + 1]
        m = m_rows[:, h:h + 1]
        dlog = jnp.where(mask, b + a_rows[h:h + 1, :], _NEG_INF)
        inter = b + m
        mt = jnp.maximum(inter, jnp.max(dlog, axis=1, keepdims=True))
        s = _dot_nt(qb, kb) * jnp.exp(dlog - mt)
        e_inter = jnp.exp(inter - mt)
        for g in range(G):
            qx[g * T:(g + 1) * T, g * HD_M:(g + 1) * HD_M] = q[g * T:(g + 1) * T, :]
            kx[g * T:(g + 1) * T, g * HD_M:(g + 1) * HD_M] = k[g * T:(g + 1) * T, :]
        c_cat = jnp.concatenate([c0_ref[0, g, h] for g in range(G)], axis=1)
        n_h = n0_ref[h]
        num = _dot(_bf(s), vb) + e_inter * _dot_nt(_bf(qx[...]), _bf(c_cat))
        den = jnp.sum(s, axis=1, keepdims=True) + e_inter * jnp.sum(q * rows_of(n_h), axis=1, keepdims=True)
        hh = num / jnp.maximum(jnp.abs(den), jnp.exp(-mt))
        h_ref[:, lanes] = jax.nn.sigmoid(o_ref[:, lanes]) * _rms(hh, gmh_ref[:, lanes])

        w = w_all[:, h:h + 1]
        upd = _dot(_bf((w * v).T), _bf(kx[...]))
        for g in range(G):
            c_out[g, h] = decay[g:g + 1, h:h + 1] * c0_ref[0, g, h] + upd[:, g * HD_M:(g + 1) * HD_M]
        n_out[h] = decay[:, h:h + 1] * n_h + jnp.sum(grp(w * k), axis=1)


def _mlstm_short(qk, vm, om, gates, b_gates, w_conv, g_mh, hist, c0, n0, m0, layer, B):
    T = SUBLANES
    G = LANES // T
    assert B % G == 0 and qk.shape[0] == B * T
    R = G * T
    row = lambda w: pl.BlockSpec((R, w), lambda i: (i, 0))
    f32 = jnp.float32
    return pl.pallas_call(
        functools.partial(_mlstm_short_kernel, group=G),
        grid=(B // G,),
        in_specs=[row(2 * D_M), row(D_M), row(D_M), row(GATE_PAD), row(2 * D_M),
                  _resident((1, GATE_PAD)), _resident((CONV_W, 2 * D_M)), _resident((1, D_M)),
                  pl.BlockSpec((1, G, H_M, HD_M, HD_M), lambda i: (layer, i, 0, 0, 0)),
                  pl.BlockSpec((H_M, G, HD_M), lambda i: (0, i, 0)),
                  pl.BlockSpec((G, GATE_PAD), lambda i: (i, 0))],
        out_specs=[row(D_M), pl.BlockSpec((G, H_M, HD_M, HD_M), lambda i: (i, 0, 0, 0)),
                   pl.BlockSpec((H_M, G, HD_M), lambda i: (0, i, 0)), pl.BlockSpec((G, GATE_PAD), lambda i: (i, 0))],
        out_shape=[jax.ShapeDtypeStruct((B * T, D_M), f32), jax.ShapeDtypeStruct((B, H_M, HD_M, HD_M), f32),
                   jax.ShapeDtypeStruct((H_M, B, HD_M), f32), jax.ShapeDtypeStruct((B, GATE_PAD), f32)],
        scratch_shapes=[pltpu.VMEM((R, G * HD_M), f32), pltpu.VMEM((R, G * HD_M), f32)],
        compiler_params=pltpu.CompilerParams(dimension_semantics=("arbitrary",), vmem_limit_bytes=VMEM_LIMIT),
        name="mlstm_short",
    )(qk, vm, om, gates, hist, b_gates, w_conv, g_mh, c0, n0, m0)


def _pattn_kernel(q_ref, kv_ref, o_ref, kvt_ref, prev_s, acc_s, m_s, l_s, s_buf, p_buf, m_buf, tok_s):
    i = pl.program_id(1)
    nb = ATT_BLOCK
    pair = 2 * HD_A
    scale = HD_A ** -0.5

    @pl.when(i == 0)
    def _():
        prev_s[...] = jnp.zeros(prev_s.shape, prev_s.dtype)

    pq = lax.broadcasted_iota(jnp.int32, (nb, nb), 0)
    pk = lax.broadcasted_iota(jnp.int32, (nb, nb), 1)
    lo_half = lax.broadcasted_iota(jnp.int32, (nb, pair), 1) < HD_A
    ones = jnp.ones((2 * nb, pair), jnp.bfloat16)
    no_keys = jnp.full((nb, nb), _NEG_INF, jnp.float32)
    npat = len(DILATED_PATTERNS)
    npair = H_A // 2

    for pi, (win, dil) in enumerate(DILATED_PATTERNS):
        first, last = pi == 0, pi == npat - 1
        nrun = ATT_STRIDE // dil
        rlen = nb // nrun
        nblk = nrun
        dist = nrun * (pq % rlen) + pq // rlen - (nrun * (pk % rlen) + pk // rlen)
        nd_cur = jnp.where(dist >= 0, -(dist * dil).astype(jnp.float32), _NEG_INF)
        nd_prev = jnp.where(dist <= 0, -((dist + nb) * dil).astype(jnp.float32), _NEG_INF)
        nd_prev_first = jnp.where(i > 0, nd_prev, no_keys)

        def starts(c, jb, dil=dil, nrun=nrun, rlen=rlen):
            return [pl.multiple_of((dil * e + c) * nb + rlen * jb, SUBLANES) for e in range(nrun)]

        def load(ref, j, st, rlen=rlen):
            return jnp.concatenate([ref[j, pl.ds(s, rlen), :] for s in st], axis=0)

        def store(ref, j, st, val, rlen=rlen):
            for e, s in enumerate(st):
                ref[j, pl.ds(s, rlen), :] = val[e * rlen:(e + 1) * rlen]

        def unit(c, jb, prev_ref, prev_jb, nd_prev_u, first=first, last=last, nd_cur=nd_cur, starts=starts,
                 load=load, store=store):
            rows = starts(c, jb)
            prows = starts(c, prev_jb)
            nd = jnp.concatenate([nd_cur, nd_prev_u], axis=1)
            for hp in range(npair):
                q2 = load(q_ref, hp, rows) * scale
                k2 = _bf(jnp.concatenate([load(kv_ref, hp, rows), load(prev_ref, hp, prows)], axis=0))
                for e in range(2):
                    qm = _bf(jnp.where(lo_half if e == 0 else jnp.logical_not(lo_half), q2, 0.0))
                    s_buf[2 * hp + e] = _dot_nt(qm, k2) + _SLOPES[2 * hp + e] * nd
            for h in range(H_A):
                s = s_buf[h]
                m = jnp.max(s, axis=1, keepdims=True)
                p_buf[h] = _bf(jnp.exp(s - m))
                m_buf[h] = jnp.broadcast_to(m, (nb, pair))
            for hp in range(npair):
                v2 = _bf(jnp.concatenate([load(kv_ref, npair + hp, rows), load(prev_ref, npair + hp, prows)], axis=0))
                rhs = jnp.concatenate([v2, ones], axis=1)
                r_e = _dot(p_buf[2 * hp], rhs)
                r_o = _dot(p_buf[2 * hp + 1], rhs)
                acc2 = jnp.where(lo_half, r_e[:, :pair], r_o[:, :pair])
                l2 = jnp.where(lo_half, r_e[:, pair:], r_o[:, pair:])
                m2 = jnp.where(lo_half, m_buf[2 * hp], m_buf[2 * hp + 1])
                if not first:
                    m_old = load(m_s, hp, rows)
                    m_new = jnp.maximum(m_old, m2)
                    a_old = jnp.exp(m_old - m_new)
                    a_loc = jnp.exp(m2 - m_new)
                    acc2 = a_old * load(acc_s, hp, rows) + a_loc * acc2
                    l2 = a_old * load(l_s, hp, rows) + a_loc * l2
                    m2 = m_new
                if last:
                    o_ref[hp, pl.ds(c, nb, stride=ATT_STRIDE), :] = acc2 / l2
                else:
                    store(acc_s, hp, rows, acc2)
                    store(m_s, hp, rows, m2)
                    store(l_s, hp, rows, l2)

        def first_block(c, carry, nblk=nblk, unit=unit, nd_prev_first=nd_prev_first):
            unit(c, 0, prev_s, nblk - 1, nd_prev_first)
            return carry

        def later_block(u, carry, dil=dil, unit=unit, nd_prev=nd_prev):
            jb = 1 + u // dil
            unit(u % dil, jb, kv_ref, jb - 1, nd_prev)
            return carry

        lax.fori_loop(0, dil, first_block, 0)
        if nblk > 1:
            lax.fori_loop(0, dil * (nblk - 1), later_block, 0)

    prev_s[...] = kv_ref[...]

    @pl.when(i == pl.num_programs(1) - 1)
    def _():
        for j in range(2 * npair):
            for r in range(ATT_STRIDE):
                tok_s[pl.ds(r, nb, stride=ATT_STRIDE), :] = kv_ref[j, r * nb:(r + 1) * nb, :]
            t = tok_s[...].T
            kvt_ref[0, j // npair, 2 * (j % npair)] = t[0:HD_A]
            kvt_ref[0, j // npair, 2 * (j % npair) + 1] = t[HD_A:pair]


def _prompt_attention(q, kv, B):
    nq, n, _ = q.shape
    S = n // B
    for win, dil in DILATED_PATTERNS:
        assert win // dil == ATT_BLOCK and ATT_TILE % (dil * ATT_BLOCK) == 0
    assert S % ATT_TILE == 0 and LANES == 2 * HD_A
    assert DILATED_PATTERNS[-1][1] == ATT_STRIDE and all(ATT_STRIDE % d == 0 for _, d in DILATED_PATTERNS)
    nt = S // ATT_TILE
    tile = lambda k: pl.BlockSpec((k, ATT_TILE, LANES), lambda b, i: (0, b * nt + i, 0))
    f32 = jnp.float32
    return pl.pallas_call(
        _pattn_kernel,
        grid=(B, nt),
        in_specs=[tile(nq), tile(2 * nq)],
        out_specs=[tile(nq), pl.BlockSpec((1, 2, H_A, HD_A, ATT_TILE), lambda b, i: (b, 0, 0, 0, 0))],
        out_shape=[jax.ShapeDtypeStruct((nq, n, LANES), f32),
                   jax.ShapeDtypeStruct((B, 2, H_A, HD_A, ATT_TILE), f32)],
        scratch_shapes=[pltpu.VMEM((2 * nq, ATT_TILE, LANES), f32), pltpu.VMEM((nq, ATT_TILE, LANES), f32),
                        pltpu.VMEM((nq, ATT_TILE, LANES), f32), pltpu.VMEM((nq, ATT_TILE, LANES), f32),
                        pltpu.VMEM((H_A, ATT_BLOCK, 2 * ATT_BLOCK), f32),
                        pltpu.VMEM((H_A, ATT_BLOCK, 2 * ATT_BLOCK), jnp.bfloat16),
                        pltpu.VMEM((H_A, ATT_BLOCK, LANES), f32), pltpu.VMEM((ATT_TILE, LANES), f32)],
        compiler_params=pltpu.CompilerParams(dimension_semantics=("arbitrary", "arbitrary"),
                                             vmem_limit_bytes=VMEM_LIMIT),
        name="pattn",
    )(q, kv)


def _sample_tables(wb, t_new):
    rows = np.arange(H_A * t_new)
    t = rows % t_new
    slope = np.asarray(_SLOPES)[rows // t_new]
    pos = np.arange(wb + t_new)
    d = wb + t[:, None] - pos[None, :]
    cnt = np.zeros(d.shape, np.float32)
    for win, dil in DILATED_PATTERNS:
        cnt += ((d >= 0) & (d % dil == 0) & (d <= win)).astype(np.float32)
    bias = np.where(cnt > 0, -slope[:, None] * d, -np.inf).astype(np.float32)
    return bias, cnt


def _outffn_kernel(*refs, guests):
    x_ref, hm_ref, ha_ref, wo_ref, g1_ref, g2_ref, g3_ref, wg_ref, wu_ref, wd_ref = refs[:10]
    if guests:
        sq_ref, skv_ref, cache_ref, bias_ref, cnt_ref, biasn_ref, cntn_ref, y_ref, sha_ref = refs[10:]
    else:
        y_ref, = refs[10:]
    ha = jnp.concatenate([ha_ref[j] for j in range(D_A // LANES)], axis=1)
    mix = _dot(_bf(hm_ref[...]), wo_ref[0:D_M, :]) + _dot(_bf(ha), wo_ref[D_M:D_M + D_A, :])
    x1 = x_ref[...] + _rms(mix, g1_ref[...])
    hf = _bf(_rms(x1, g2_ref[...]))
    gate = _dot(hf, wg_ref[...])
    for b in range(guests):
        _sample_attention_rows(sq_ref, skv_ref, cache_ref, bias_ref, cnt_ref, biasn_ref, cntn_ref, sha_ref, b)
    up = _dot(hf, wu_ref[...])
    f = _dot(_bf(gate * jax.nn.sigmoid(gate) * up), wd_ref[...])
    y_ref[...] = x1 + _rms(f, g3_ref[...])


def _outffn(x, hm, ha, w_out, g_mix_post, g_ffn_pre, g_ffn_post, w_gate, w_up, w_down, tm, guest=None):
    n, d = x.shape
    row = lambda w: pl.BlockSpec((tm, w), lambda i: (i, 0))
    in_specs = [row(d), row(D_M), pl.BlockSpec((D_A // LANES, tm, LANES), lambda i: (0, i, 0)),
                _resident(w_out.shape), _resident((1, d)), _resident((1, d)),
                _resident((1, d)), _resident(w_gate.shape), _resident(w_up.shape), _resident(w_down.shape)]
    args = [x, hm, ha, w_out, g_mix_post, g_ffn_pre, g_ffn_post, w_gate, w_up, w_down]
    out_specs = [row(d)]
    out_shape = [jax.ShapeDtypeStruct((n, d), jnp.float32)]
    guests = 0
    if guest is not None:
        guests, g_specs, g_args, g_out_spec, g_out_shape = _guest_plumbing(guest, n // tm)
        in_specs += g_specs
        args += g_args
        out_specs.append(g_out_spec)
        out_shape.append(g_out_shape)
    res = pl.pallas_call(
        functools.partial(_outffn_kernel, guests=guests),
        grid=(n // tm,),
        in_specs=in_specs,
        out_specs=out_specs,
        out_shape=out_shape,
        compiler_params=pltpu.CompilerParams(dimension_semantics=("arbitrary",), vmem_limit_bytes=VMEM_LIMIT),
        name="outffn",
    )(*args)
    return res if guest is not None else res[0]


def _layer(xp, xs, lw, conv_s, c_all, n_s, m_s, cache, layer):
    B, S, D = xp.shape
    BS, T, _ = xs.shape
    assert T == SUBLANES and BS % (LANES // SUBLANES) == 0 and MAX_WINDOW == ATT_TILE
    f32 = jnp.float32
    hist = CONV_W - 1
    n_p, n_sm = B * S, BS * T
    xpf, xsf = xp.reshape(n_p, D), xs.reshape(n_sm, D)
    rows = lambda a, b, t: jnp.transpose(a, (1, 0, 2)).reshape(b, t, a.shape[0] * LANES)
    ffn = lambda x, hm, ha, tm, guest=None: _outffn(
        x, hm, ha, lw["w_out"], lw["g_mix_post"], lw["g_ffn_pre"], lw["g_ffn_post"], lw["w_gate"], lw["w_up"],
        lw["w_down"], tm, guest)
    half = BS // 2

    qk_s, vm_s, om_s, qa_s, kva_s, gates_s = _inproj(xsf, lw["g_mix_pre"], lw["w_main"], lw["w_gates"],
                                                     _row_tile(n_sm, 1024))
    kv_rows = rows(kva_s, BS, T)
    q_rows = rows(qa_s, BS, T)
    qk, vm, om, qa, kva, gates, ha_s0 = _inproj(xpf, lw["g_mix_pre"], lw["w_main"], lw["w_gates"],
                                                _row_tile(S, HOST_ROWS), guest=(q_rows, kv_rows, cache, layer, 0, half))

    r3 = lambda a: a.reshape(B, S, a.shape[-1])
    hm, c_p, n_p_new, m_p, conv_p = _mlstm(
        r3(qk), r3(vm), r3(om), r3(gates), lw["b_gates"], lw["w_conv"], lw["g_mh"],
        jnp.zeros((B, hist, 2 * D_M), f32), jnp.zeros((1, B, H_M, HD_M, HD_M), f32),
        jnp.zeros((B, H_M, 1, HD_M), f32), jnp.zeros((B, H_M, 1, LANES), f32), 0)
    ha, kv_t = _prompt_attention(qa, kva, B)
    yp, ha_s1 = ffn(xpf, hm.reshape(n_p, D_M), ha, _row_tile(S, HOST_ROWS),
                    (q_rows, kv_rows, cache, layer, half, BS - half))
    ha_s = jnp.concatenate([ha_s0, ha_s1], axis=0)
    out_p = (jnp.transpose(kv_t, (0, 4, 1, 2, 3)),
             c_p, n_p_new.reshape(B, H_M, HD_M), m_p[:, :, 0, 0], conv_p)

    conv_rows = jnp.pad(conv_s, ((0, 0), (0, T - hist), (0, 0))).reshape(n_sm, 2 * D_M)
    hm_s, c_s, n_s_new, m_s_new = _mlstm_short(
        qk_s, vm_s, om_s, gates_s, lw["b_gates"], lw["w_conv"], lw["g_mh"], conv_rows, c_all,
        jnp.transpose(n_s, (1, 0, 2)), jnp.pad(m_s, ((0, 0), (0, GATE_PAD - H_M))), layer, BS)
    ys = ffn(xsf, hm_s, jnp.transpose(ha_s.reshape(n_sm, D_A // LANES, LANES), (1, 0, 2)), _row_tile(n_sm, 512))
    out_s = (kv_rows.reshape(BS, T, 2, H_A, HD_A), c_s, jnp.transpose(n_s_new, (1, 0, 2)), m_s_new[:, :H_M],
             qk_s.reshape(BS, T, 2 * D_M)[:, T - hist:])
    return yp.reshape(B, S, D), ys.reshape(BS, T, D), out_p, out_s


def kernel(x_prompt, x_sample, cache_kv, state_C, state_n, state_m, state_conv, g_mix_pre, g_mix_post, g_ffn_pre,
           g_ffn_post, w_in, b_gates, w_conv, g_mh, w_out, w_gate, w_up, w_down):
    depth = w_in.shape[0]
    g0 = 4 * D_M
    g1 = g0 + 2 * H_M
    cache = jnp.transpose(cache_kv, (0, 1, 3, 4, 5, 2))
    w_in_t = _bf(jnp.transpose(w_in, (0, 2, 1)))

    hp, hs = x_prompt, x_sample
    outs_p, outs_s = [], []
    for l in range(depth):
        lw = {
            "g_mix_pre": g_mix_pre[l][None], "g_mix_post": g_mix_post[l][None],
            "g_ffn_pre": g_ffn_pre[l][None], "g_ffn_post": g_ffn_post[l][None],
            "w_main": jnp.concatenate([w_in_t[l, :g0], w_in_t[l, g1:]], axis=0),
            "w_gates": jnp.pad(w_in_t[l, g0:g1], ((0, GATE_PAD - 2 * H_M), (0, 0))),
            "b_gates": jnp.pad(b_gates[l], (0, GATE_PAD - 2 * H_M))[None],
            "w_conv": w_conv[l], "g_mh": g_mh[l][None],
            "w_out": _bf(w_out[l]), "w_gate": _bf(w_gate[l]), "w_up": _bf(w_up[l]), "w_down": _bf(w_down[l]),
        }
        hp, hs, out_p, out_s = _layer(hp, hs, lw, state_conv[l], state_C, state_n[l], state_m[l], cache, l)
        outs_p.append(out_p)
        outs_s.append(out_s)
    stack = lambda outs, i: jnp.stack([o[i] for o in outs])
    return (hp, hs, stack(outs_p, 0), stack(outs_s, 0),
            stack(outs_p, 1), stack(outs_p, 2), stack(outs_p, 3), stack(outs_p, 4),
            stack(outs_s, 1), stack(outs_s, 2), stack(outs_s, 3), stack(outs_s, 4))
```

```python
import functools

import numpy as np
import jax
import jax.numpy as jnp
from jax import lax
from jax.experimental import pallas as pl
from jax.experimental.pallas import tpu as pltpu

H_M = 5
HD_M = 128
D_M = H_M * HD_M
H_A = 6
HD_A = 64
D_A = H_A * HD_A
CONV_W = 4
MLSTM_MAX_CHUNK = 256
MLSTM_GROUP_TOKENS = 256
MLSTM_MAX_GROUP = 4
DILATED_PATTERNS = ((128, 1), (512, 4), (2048, 16))
MAX_WINDOW = 2048
EPS = 1e-6
LANES = 128
GATE_PAD = LANES
HOST_ROWS = 256
SUBLANES = 8
ATT_BLOCK = 128
ATT_STRIDE = max(d for _, d in DILATED_PATTERNS)
ATT_TILE = ATT_BLOCK * ATT_STRIDE
VMEM_LIMIT = 56 * 1024 * 1024

_SLOPES = [2.0 ** (-8.0 * (h + 1) / H_A) for h in range(H_A)]
_NEG_INF = float("-inf")


def _bf(x):
    return x.astype(jnp.bfloat16)


def _dot(a, b):
    return jnp.dot(a, b, preferred_element_type=jnp.float32)


def _dot_nt(a, b):
    return lax.dot_general(a, b, (((1,), (1,)), ((), ())), preferred_element_type=jnp.float32)


def _rms(x, g):
    return x * lax.rsqrt(jnp.mean(x * x, axis=-1, keepdims=True) + EPS) * g


def _resident(shape):
    nd = len(shape)
    return pl.BlockSpec(shape, lambda *_: (0,) * nd, pipeline_mode=pl.Buffered(1))


def _row_tile(n, cap):
    t = min(n, cap)
    while n % t:
        t //= 2
    return t


_IN_CUTS = (0, 2 * D_M, 3 * D_M, 4 * D_M, 4 * D_M + D_A, 4 * D_M + 3 * D_A)


_CONV_PAD = 8


def _sample_attention_rows(q_ref, kvn_ref, cache_ref, bias_ref, cnt_ref, biasn_ref, cntn_ref, o_ref, b):
    t_new = q_ref.shape[1]
    scale = HD_A ** -0.5
    heads = [(h * HD_A, (h + 1) * HD_A) for h in range(H_A)]
    qh = [_bf(q_ref[b, :, lo:hi]) for lo, hi in heads]
    sc = jnp.concatenate([_dot(qh[h], _bf(cache_ref[0, b, 0, h])) for h in range(H_A)], axis=0)
    sn = jnp.concatenate([_dot_nt(qh[h], _bf(kvn_ref[b, :, lo:hi])) for h, (lo, hi) in enumerate(heads)], axis=0)
    sc = sc * scale + bias_ref[...]
    sn = sn * scale + biasn_ref[...]
    mx = jnp.maximum(jnp.max(sc, axis=1, keepdims=True), jnp.max(sn, axis=1, keepdims=True))
    pc = cnt_ref[...] * jnp.exp(sc - mx)
    pn = cntn_ref[...] * jnp.exp(sn - mx)
    den = jnp.sum(pc, axis=1, keepdims=True) + jnp.sum(pn, axis=1, keepdims=True)
    outs = []
    for h, (lo, hi) in enumerate(heads):
        r0, r1 = h * t_new, (h + 1) * t_new
        o = (_dot_nt(_bf(pc[r0:r1]), _bf(cache_ref[0, b, 1, h]))
             + _dot(_bf(pn[r0:r1]), _bf(kvn_ref[b, :, D_A + lo:D_A + hi])))
        outs.append(o / den[r0:r1])
    o_ref[b] = jnp.concatenate(outs, axis=1)


def _inproj_kernel(*refs, tm, guests):
    if not guests:
        x_ref, g_ref, w_ref, wg_ref, qk_ref, vm_ref, om_ref, qa_ref, kva_ref, gates_ref = refs
    else:
        (x_ref, g_ref, w_ref, wg_ref, sq_ref, skv_ref, cache_ref, bias_ref, cnt_ref, biasn_ref, cntn_ref,
         qk_ref, vm_ref, om_ref, qa_ref, kva_ref, gates_ref, sha_ref, zs) = refs
    a = _bf(_rms(x_ref[...], g_ref[...]))
    nq = D_A // LANES
    att_tile = lambda j: (qa_ref, j) if j < nq else (kva_ref, j - nq)
    z = _dot_nt(a, w_ref[_IN_CUTS[3]:_IN_CUTS[5], :])
    if not guests:
        for j in range(3 * nq):
            dst, jj = att_tile(j)
            dst[jj] = z[:, j * LANES:(j + 1) * LANES]
    else:
        for j in range(3 * nq):
            zs[j] = z[:, j * LANES:(j + 1) * LANES]
    qk_ref[...] = _dot_nt(a, w_ref[_IN_CUTS[0]:_IN_CUTS[1], :])
    if guests:
        sub = pl.program_id(0) % (ATT_TILE // tm)
        per = tm // ATT_STRIDE
        for j in range(3 * nq):
            dst, jj = att_tile(j)
            for r in range(ATT_STRIDE):
                start = pl.multiple_of(r * ATT_BLOCK + sub * per, SUBLANES)
                dst[jj, pl.ds(start, per), :] = zs[j, pl.ds(r, per, stride=ATT_STRIDE), :]
    vm_ref[...] = _dot_nt(a, w_ref[_IN_CUTS[1]:_IN_CUTS[2], :])
    for b in range(guests):
        _sample_attention_rows(sq_ref, skv_ref, cache_ref, bias_ref, cnt_ref, biasn_ref, cntn_ref, sha_ref, b)
    om_ref[...] = _dot_nt(a, w_ref[_IN_CUTS[2]:_IN_CUTS[3], :])
    gates_ref[...] = _dot_nt(a, wg_ref[...])


def _guest_plumbing(guest, steps):
    sq, skv, cache_t, layer, first, count = guest
    T = sq.shape[1]
    wb = cache_t.shape[-1]
    assert count % steps == 0 and first % (count // steps) == 0
    per_step = count // steps
    off = first // per_step
    bias, cnt = _sample_tables(wb, T)
    consts = [bias[:, :wb], cnt[:, :wb], bias[:, wb:], cnt[:, wb:]]
    per_g = lambda w: pl.BlockSpec((per_step, T, w), lambda i: (off + i, 0, 0))
    in_specs = [per_g(D_A), per_g(2 * D_A),
                pl.BlockSpec((1, per_step, 2, H_A, HD_A, wb), lambda i: (layer, off + i, 0, 0, 0, 0))]
    in_specs += [_resident(c.shape) for c in consts]
    args = [sq, skv, cache_t] + [jnp.asarray(c) for c in consts]
    out_spec = pl.BlockSpec((per_step, T, D_A), lambda i: (i, 0, 0))
    return per_step, in_specs, args, out_spec, jax.ShapeDtypeStruct((count, T, D_A), jnp.float32)


def _inproj(x, g, w_main, w_gates, tm, guest=None):
    n, d = x.shape
    row = lambda w: pl.BlockSpec((tm, w), lambda i: (i, 0))
    f32 = jnp.float32
    in_specs = [row(d), _resident((1, d)), _resident(w_main.shape), _resident(w_gates.shape)]
    out_shape = [jax.ShapeDtypeStruct((n, 2 * D_M), f32), jax.ShapeDtypeStruct((n, D_M), f32),
                 jax.ShapeDtypeStruct((n, D_M), f32), jax.ShapeDtypeStruct((D_A // LANES, n, LANES), f32),
                 jax.ShapeDtypeStruct((2 * D_A // LANES, n, LANES), f32), jax.ShapeDtypeStruct((n, GATE_PAD), f32)]
    args = [x, g, w_main, w_gates]
    scratch = []
    guests = 0
    if guest is None:
        tiles = lambda w: pl.BlockSpec((w // LANES, tm, LANES), lambda i: (0, i, 0))
        out_specs = [row(2 * D_M), row(D_M), row(D_M), tiles(D_A), tiles(2 * D_A), row(GATE_PAD)]
    else:
        assert n % ATT_TILE == 0 and ATT_TILE % tm == 0 and tm % (ATT_STRIDE * SUBLANES) == 0
        guests, g_specs, g_args, g_out_spec, g_out_shape = _guest_plumbing(guest, n // tm)
        per_att = ATT_TILE // tm
        tiles = lambda w: pl.BlockSpec((w // LANES, ATT_TILE, LANES), lambda i: (0, i // per_att, 0),
                                       pipeline_mode=pl.Buffered(1))
        in_specs += g_specs
        out_specs = [row(2 * D_M), row(D_M), row(D_M), tiles(D_A), tiles(2 * D_A), row(GATE_PAD), g_out_spec]
        out_shape.append(g_out_shape)
        args += g_args
        scratch = [pltpu.VMEM((3 * D_A // LANES, tm, LANES), f32)]
    return pl.pallas_call(
        functools.partial(_inproj_kernel, tm=tm, guests=guests),
        grid=(n // tm,),
        in_specs=in_specs,
        out_specs=out_specs,
        out_shape=out_shape,
        scratch_shapes=scratch,
        compiler_params=pltpu.CompilerParams(dimension_semantics=("arbitrary",), vmem_limit_bytes=VMEM_LIMIT),
        name="inproj",
    )(*args)


def _mlstm_head(q, k, v, a_col, b, a_row, m, cmat, nvec, causal, eye):
    L = q.shape[0]
    qb, kb, vb = _bf(q), _bf(k), _bf(v)
    if a_row is None:
        a_row = jnp.sum(jnp.where(eye, a_col, 0.0), axis=0, keepdims=True)
    dlog = jnp.where(causal, b + a_row, _NEG_INF)
    inter = b + m
    mt = jnp.maximum(inter, jnp.max(dlog, axis=1, keepdims=True))
    s = _dot_nt(qb, kb) * jnp.exp(dlog - mt)
    e_inter = jnp.exp(inter - mt)
    num = _dot(_bf(s), vb) + e_inter * _dot_nt(qb, _bf(cmat))
    den = jnp.sum(s, axis=1, keepdims=True) + e_inter * jnp.sum(q * nvec, axis=1, keepdims=True)
    hh = num / jnp.maximum(jnp.abs(den), jnp.exp(-mt))

    b_last = b[L - 1:L, :]
    wlog = b_last + a_col
    m_new = jnp.maximum(b_last + m, jnp.max(wlog, axis=0, keepdims=True))
    w = jnp.exp(wlog - m_new)
    decay = jnp.exp(b_last + m - m_new)
    wv = w * v
    if L < HD_M:
        pad = jnp.zeros((HD_M - L, HD_M), jnp.float32)
        wv_t = _bf(jnp.concatenate([wv, pad], axis=0).T)
        k_pad = _bf(jnp.concatenate([k, pad], axis=0))
    else:
        wv_t = _bf(wv.T)
        k_pad = kb
    c_new = decay * cmat + _dot(wv_t, k_pad)
    n_new = decay * nvec + jnp.sum(w * k, axis=0, keepdims=True)
    return hh, c_new, n_new, m_new


def _mlstm_kernel(qk_ref, v_ref, o_ref, g_ref, bias_ref, wconv_ref, gmh_ref, conv0_ref, c0_ref, n0_ref, m0_ref,
                  h_ref, c_out, n_out, m_out, conv_out, ubuf, c_s, n_s, m_s, *, chunk, group):
    L = chunk
    c = pl.program_id(1)
    hist = CONV_W - 1

    @pl.when(c == 0)
    def _():
        ubuf[:, _CONV_PAD - hist:_CONV_PAD, :] = conv0_ref[...]
        c_s[...] = c0_ref[0]
        n_s[...] = n0_ref[...]
        m_s[...] = m0_ref[...]

    row_id = lax.broadcasted_iota(jnp.int32, (L, GATE_PAD), 0)
    ti = lax.broadcasted_iota(jnp.int32, (L, L), 0)
    si = lax.broadcasted_iota(jnp.int32, (L, L), 1)
    causal = si <= ti
    eye = si == ti

    for g in range(group):
        ubuf[g, _CONV_PAD:_CONV_PAD + L, :] = qk_ref[g]
        conv = ubuf[g, _CONV_PAD - hist:_CONV_PAD - hist + L, :] * wconv_ref[0:1, :]
        for i in range(1, CONV_W):
            conv = conv + ubuf[g, _CONV_PAD - hist + i:_CONV_PAD - hist + i + L, :] * wconv_ref[i:i + 1, :]
        tail = ubuf[g, _CONV_PAD + L - hist:_CONV_PAD + L, :]
        ubuf[g, _CONV_PAD - hist:_CONV_PAD, :] = tail
        conv_out[g] = tail
        qk = conv * jax.nn.sigmoid(conv)

        gates = g_ref[g] + bias_ref[...]
        lf = jnp.minimum(gates, 0.0) - jnp.log(1.0 + jnp.exp(-jnp.abs(gates)))
        bcum = lf
        step = 1
        while step < L:
            bcum = bcum + jnp.where(row_id >= step, pltpu.roll(bcum, step, 0), 0.0)
            step *= 2

        a_all = gates - pltpu.roll(bcum, GATE_PAD - H_M, 1)
        a_rows = a_all.T if L % LANES == 0 else None

        for h in range(H_M):
            lanes = slice(h * HD_M, (h + 1) * HD_M)
            hh, c_new, n_new, m_new = _mlstm_head(
                qk[:, lanes], qk[:, D_M + h * HD_M:D_M + (h + 1) * HD_M] * (HD_M ** -0.5), v_ref[g, :, lanes],
                a_all[:, h:h + 1], bcum[:, H_M + h:H_M + h + 1], None if a_rows is None else a_rows[h:h + 1, :],
                m_s[g, h][:, 0:1], c_s[g, h], n_s[g, h], causal, eye)
            c_s[g, h] = c_new
            n_s[g, h] = n_new
            m_s[g, h] = jnp.broadcast_to(m_new, (1, LANES))
            h_ref[g, :, lanes] = jax.nn.sigmoid(o_ref[g, :, lanes]) * _rms(hh, gmh_ref[:, lanes])

    @pl.when(c == pl.num_programs(1) - 1)
    def _():
        c_out[...] = c_s[...]
        n_out[...] = n_s[...]
        m_out[...] = m_s[...]


def _mlstm_tiling(B, T):
    chunk = _row_tile(T, MLSTM_MAX_CHUNK)
    group = _row_tile(B, max(1, min(MLSTM_MAX_GROUP, MLSTM_GROUP_TOKENS // chunk)))
    return chunk, group


def _mlstm(qk, vm, om, gates, b_gates, w_conv, g_mh, conv0, c0, n0, m0, layer):
    B, T, _ = qk.shape
    chunk, G = _mlstm_tiling(B, T)
    tok = lambda w: pl.BlockSpec((G, chunk, w), lambda b, c: (b, c, 0))
    per_b = lambda shape: pl.BlockSpec((G,) + shape, lambda b, c: (b,) + (0,) * len(shape))
    hist = CONV_W - 1
    f32 = jnp.float32
    return pl.pallas_call(
        functools.partial(_mlstm_kernel, chunk=chunk, group=G),
        grid=(B // G, T // chunk),
        in_specs=[tok(2 * D_M), tok(D_M), tok(D_M), tok(GATE_PAD),
                  _resident((1, GATE_PAD)), _resident((CONV_W, 2 * D_M)), _resident((1, D_M)),
                  per_b((hist, 2 * D_M)),
                  pl.BlockSpec((1, G, H_M, HD_M, HD_M), lambda b, c: (layer, b, 0, 0, 0)),
                  per_b((H_M, 1, HD_M)), per_b((H_M, 1, LANES))],
        out_specs=[tok(D_M), per_b((H_M, HD_M, HD_M)), per_b((H_M, 1, HD_M)), per_b((H_M, 1, LANES)),
                   per_b((hist, 2 * D_M))],
        out_shape=[jax.ShapeDtypeStruct((B, T, D_M), f32),
                   jax.ShapeDtypeStruct((B, H_M, HD_M, HD_M), f32),
                   jax.ShapeDtypeStruct((B, H_M, 1, HD_M), f32),
                   jax.ShapeDtypeStruct((B, H_M, 1, LANES), f32),
                   jax.ShapeDtypeStruct((B, hist, 2 * D_M), f32)],
        scratch_shapes=[pltpu.VMEM((G, _CONV_PAD + chunk, 2 * D_M), f32),
                        pltpu.VMEM((G, H_M, HD_M, HD_M), f32),
                        pltpu.VMEM((G, H_M, 1, HD_M), f32),
                        pltpu.VMEM((G, H_M, 1, LANES), f32)],
        compiler_params=pltpu.CompilerParams(dimension_semantics=("arbitrary", "arbitrary"),
                                             vmem_limit_bytes=VMEM_LIMIT),
        name="mlstm",
    )(qk, vm, om, gates, b_gates, w_conv, g_mh, conv0, c0, n0, m0)


def _mlstm_short_kernel(qk_ref, v_ref, o_ref, g_ref, hist_ref, bias_ref, wconv_ref, gmh_ref, c0_ref, n0_ref, m0_ref,
                        c_alias_ref, h_ref, c_out, n_out, m_out, qx, kx, *, group):
    G, T = group, SUBLANES
    R = G * T
    f32 = jnp.float32

    @pl.when(pl.program_id(0) == 0)
    def _():
        qx[...] = jnp.zeros(qx.shape, f32)
        kx[...] = jnp.zeros(kx.shape, f32)

    grp = lambda x: x.reshape(G, T, x.shape[-1])
    rows_of = lambda x: jnp.broadcast_to(x[:, None, :], (G, T, x.shape[-1])).reshape(R, x.shape[-1])
    shift_gate = lambda x: pltpu.roll(x, GATE_PAD - H_M, 1)

    t_wide = lax.broadcasted_iota(jnp.int32, (R, 2 * D_M), 0) % T
    u = qk_ref[...]
    hist = hist_ref[...]
    conv = u * wconv_ref[CONV_W - 1:CONV_W, :]
    for k in range(1, CONV_W):
        back = jnp.where(t_wide >= k, pltpu.roll(u, k, 0), pltpu.roll(hist, (R - (CONV_W - 1 - k)) % R, 0))
        conv = conv + back * wconv_ref[CONV_W - 1 - k:CONV_W - k, :]
    qk = conv * jax.nn.sigmoid(conv)

    gates = g_ref[...] + bias_ref[...]
    lf = jnp.minimum(gates, 0.0) - jnp.log(1.0 + jnp.exp(-jnp.abs(gates)))
    t_id = lax.broadcasted_iota(jnp.int32, (R, GATE_PAD), 0) % T
    bcum = lf
    step = 1
    while step < T:
        bcum = bcum + jnp.where(t_id >= step, pltpu.roll(bcum, step, 0), 0.0)
        step *= 2

    a_all = gates - shift_gate(bcum)
    a_rows = a_all.T
    b_last = shift_gate(rows_of(grp(bcum)[:, T - 1, :]))
    m_rows = rows_of(m0_ref[...])
    wlog = b_last + a_all
    m_new = jnp.maximum(b_last + m_rows, rows_of(jnp.max(grp(wlog), axis=1)))
    w_all = jnp.exp(wlog - m_new)
    decay = grp(jnp.exp(b_last + m_rows - m_new))[:, 0, :]
    m_out[...] = grp(m_new)[:, 0, :]

    ri = lax.broadcasted_iota(jnp.int32, (R, R), 0)
    ci = lax.broadcasted_iota(jnp.int32, (R, R), 1)
    mask = (ri // T == ci // T) & (ci <= ri)

    for h in range(H_M):
        lanes = slice(h * HD_M, (h + 1) * HD_M)
        q = qk[:, lanes]
        k = qk[:, D_M + h * HD_M:D_M + (h + 1) * HD_M] * (HD_M ** -0.5)
        v = v_ref[:, lanes]
        qb, kb, vb = _bf(q), _bf(k), _bf(v)
        b = bcum[:, H_M + h:H_M + h + 1]
        m = m_rows[:, h:h + 1]
        dlog = jnp.where(mask, b + a_rows[h:h + 1, :], _NEG_INF)
        inter = b + m
        mt = jnp.maximum(inter, jnp.max(dlog, axis=1, keepdims=True))
        s = _dot_nt(qb, kb) * jnp.exp(dlog - mt)
        e_inter = jnp.exp(inter - mt)
        for g in range(G):
            qx[g * T:(g + 1) * T, g * HD_M:(g + 1) * HD_M] = q[g * T:(g + 1) * T, :]
            kx[g * T:(g + 1) * T, g * HD_M:(g + 1) * HD_M] = k[g * T:(g + 1) * T, :]
        c_cat = jnp.concatenate([c0_ref[0, g, h] for g in range(G)], axis=1)
        n_h = n0_ref[h]
        num = _dot(_bf(s), vb) + e_inter * _dot_nt(_bf(qx[...]), _bf(c_cat))
        den = jnp.sum(s, axis=1, keepdims=True) + e_inter * jnp.sum(q * rows_of(n_h), axis=1, keepdims=True)
        hh = num / jnp.maximum(jnp.abs(den), jnp.exp(-mt))
        h_ref[:, lanes] = jax.nn.sigmoid(o_ref[:, lanes]) * _rms(hh, gmh_ref[:, lanes])

        w = w_all[:, h:h + 1]
        upd = _dot(_bf((w * v).T), _bf(kx[...]))
        for g in range(G):
            c_out[0, g, h] = decay[g:g + 1, h:h + 1] * c0_ref[0, g, h] + upd[:, g * HD_M:(g + 1) * HD_M]
        n_out[h] = decay[:, h:h + 1] * n_h + jnp.sum(grp(w * k), axis=1)


def _mlstm_short(qk, vm, om, gates, b_gates, w_conv, g_mh, hist, c0, n0, m0, layer, B, c_all_new):
    T = SUBLANES
    G = LANES // T
    assert B % G == 0 and qk.shape[0] == B * T
    R = G * T
    row = lambda w: pl.BlockSpec((R, w), lambda i: (i, 0))
    f32 = jnp.float32
    in_specs = [row(2 * D_M), row(D_M), row(D_M), row(GATE_PAD), row(2 * D_M),
                _resident((1, GATE_PAD)), _resident((CONV_W, 2 * D_M)), _resident((1, D_M)),
                pl.BlockSpec((1, G, H_M, HD_M, HD_M), lambda i: (layer, i, 0, 0, 0)),
                pl.BlockSpec((H_M, G, HD_M), lambda i: (0, i, 0)),
                pl.BlockSpec((G, GATE_PAD), lambda i: (i, 0))]
    args = [qk, vm, om, gates, hist, b_gates, w_conv, g_mh, c0, n0, m0, c_all_new]
    aliases = {len(args) - 1: 1}
    in_specs.append(pl.BlockSpec(memory_space=pl.ANY))
    return pl.pallas_call(
        functools.partial(_mlstm_short_kernel, group=G),
        grid=(B // G,),
        in_specs=in_specs,
        out_specs=[row(D_M), pl.BlockSpec((1, G, H_M, HD_M, HD_M), lambda i: (layer, i, 0, 0, 0)),
                   pl.BlockSpec((H_M, G, HD_M), lambda i: (0, i, 0)), pl.BlockSpec((G, GATE_PAD), lambda i: (i, 0))],
        out_shape=[jax.ShapeDtypeStruct((B * T, D_M), f32), jax.ShapeDtypeStruct(c0.shape, f32),
                   jax.ShapeDtypeStruct((H_M, B, HD_M), f32), jax.ShapeDtypeStruct((B, GATE_PAD), f32)],
        input_output_aliases=aliases,
        scratch_shapes=[pltpu.VMEM((R, G * HD_M), f32), pltpu.VMEM((R, G * HD_M), f32)],
        compiler_params=pltpu.CompilerParams(dimension_semantics=("arbitrary",), vmem_limit_bytes=VMEM_LIMIT),
        name="mlstm_short",
    )(*args)


def _pattn_kernel(q_ref, kv_ref, kvt_alias_ref, o_ref, kvt_ref, prev_s, acc_s, m_s, l_s, s_buf, p_buf, m_buf, tok_s):
    i = pl.program_id(1)
    nb = ATT_BLOCK
    pair = 2 * HD_A
    scale = HD_A ** -0.5

    @pl.when(i == 0)
    def _():
        prev_s[...] = jnp.zeros(prev_s.shape, prev_s.dtype)

    pq = lax.broadcasted_iota(jnp.int32, (nb, nb), 0)
    pk = lax.broadcasted_iota(jnp.int32, (nb, nb), 1)
    lo_half = lax.broadcasted_iota(jnp.int32, (nb, pair), 1) < HD_A
    ones = jnp.ones((2 * nb, pair), jnp.bfloat16)
    no_keys = jnp.full((nb, nb), _NEG_INF, jnp.float32)
    npat = len(DILATED_PATTERNS)
    npair = H_A // 2

    for pi, (win, dil) in enumerate(DILATED_PATTERNS):
        first, last = pi == 0, pi == npat - 1
        nrun = ATT_STRIDE // dil
        rlen = nb // nrun
        nblk = nrun
        dist = nrun * (pq % rlen) + pq // rlen - (nrun * (pk % rlen) + pk // rlen)
        nd_cur = jnp.where(dist >= 0, -(dist * dil).astype(jnp.float32), _NEG_INF)
        nd_prev = jnp.where(dist <= 0, -((dist + nb) * dil).astype(jnp.float32), _NEG_INF)
        nd_prev_first = jnp.where(i > 0, nd_prev, no_keys)

        def starts(c, jb, dil=dil, nrun=nrun, rlen=rlen):
            return [pl.multiple_of((dil * e + c) * nb + rlen * jb, SUBLANES) for e in range(nrun)]

        def load(ref, j, st, rlen=rlen):
            return jnp.concatenate([ref[j, pl.ds(s, rlen), :] for s in st], axis=0)

        def store(ref, j, st, val, rlen=rlen):
            for e, s in enumerate(st):
                ref[j, pl.ds(s, rlen), :] = val[e * rlen:(e + 1) * rlen]

        def unit(c, jb, prev_ref, prev_jb, nd_prev_u, first=first, last=last, nd_cur=nd_cur, starts=starts,
                 load=load, store=store):
            rows = starts(c, jb)
            prows = starts(c, prev_jb)
            nd = jnp.concatenate([nd_cur, nd_prev_u], axis=1)
            for hp in range(npair):
                q2 = load(q_ref, hp, rows) * scale
                k2 = _bf(jnp.concatenate([load(kv_ref, hp, rows), load(prev_ref, hp, prows)], axis=0))
                for e in range(2):
                    qm = _bf(jnp.where(lo_half if e == 0 else jnp.logical_not(lo_half), q2, 0.0))
                    s_buf[2 * hp + e] = _dot_nt(qm, k2) + _SLOPES[2 * hp + e] * nd
            for h in range(H_A):
                s = s_buf[h]
                m = jnp.max(s, axis=1, keepdims=True)
                p_buf[h] = _bf(jnp.exp(s - m))
                m_buf[h] = jnp.broadcast_to(m, (nb, pair))
            for hp in range(npair):
                v2 = _bf(jnp.concatenate([load(kv_ref, npair + hp, rows), load(prev_ref, npair + hp, prows)], axis=0))
                rhs = jnp.concatenate([v2, ones], axis=1)
                r_e = _dot(p_buf[2 * hp], rhs)
                r_o = _dot(p_buf[2 * hp + 1], rhs)
                acc2 = jnp.where(lo_half, r_e[:, :pair], r_o[:, :pair])
                l2 = jnp.where(lo_half, r_e[:, pair:], r_o[:, pair:])
                m2 = jnp.where(lo_half, m_buf[2 * hp], m_buf[2 * hp + 1])
                if not first:
                    m_old = load(m_s, hp, rows)
                    m_new = jnp.maximum(m_old, m2)
                    a_old = jnp.exp(m_old - m_new)
                    a_loc = jnp.exp(m2 - m_new)
                    acc2 = a_old * load(acc_s, hp, rows) + a_loc * acc2
                    l2 = a_old * load(l_s, hp, rows) + a_loc * l2
                    m2 = m_new
                if last:
                    o_ref[hp, pl.ds(c, nb, stride=ATT_STRIDE), :] = acc2 / l2
                else:
                    store(acc_s, hp, rows, acc2)
                    store(m_s, hp, rows, m2)
                    store(l_s, hp, rows, l2)

        def first_block(c, carry, nblk=nblk, unit=unit, nd_prev_first=nd_prev_first):
            unit(c, 0, prev_s, nblk - 1, nd_prev_first)
            return carry

        def later_block(u, carry, dil=dil, unit=unit, nd_prev=nd_prev):
            jb = 1 + u // dil
            unit(u % dil, jb, kv_ref, jb - 1, nd_prev)
            return carry

        lax.fori_loop(0, dil, first_block, 0)
        if nblk > 1:
            lax.fori_loop(0, dil * (nblk - 1), later_block, 0)

    prev_s[...] = kv_ref[...]

    @pl.when(i == pl.num_programs(1) - 1)
    def _():
        for j in range(2 * npair):
            for r in range(ATT_STRIDE):
                tok_s[pl.ds(r, nb, stride=ATT_STRIDE), :] = kv_ref[j, r * nb:(r + 1) * nb, :]
            t = tok_s[...].T
            kvt_ref[0, 0, j // npair, 2 * (j % npair)] = t[0:HD_A]
            kvt_ref[0, 0, j // npair, 2 * (j % npair) + 1] = t[HD_A:pair]


def _prompt_attention(q, kv, B, layer, depth, kvt_all):
    nq, n, _ = q.shape
    S = n // B
    for win, dil in DILATED_PATTERNS:
        assert win // dil == ATT_BLOCK and ATT_TILE % (dil * ATT_BLOCK) == 0
    assert S % ATT_TILE == 0 and LANES == 2 * HD_A
    assert DILATED_PATTERNS[-1][1] == ATT_STRIDE and all(ATT_STRIDE % d == 0 for _, d in DILATED_PATTERNS)
    nt = S // ATT_TILE
    tile = lambda k: pl.BlockSpec((k, ATT_TILE, LANES), lambda b, i: (0, b * nt + i, 0))
    f32 = jnp.float32
    in_specs = [tile(nq), tile(2 * nq), pl.BlockSpec(memory_space=pl.ANY)]
    args = [q, kv, kvt_all]
    aliases = {2: 1}
    return pl.pallas_call(
        _pattn_kernel,
        grid=(B, nt),
        in_specs=in_specs,
        out_specs=[tile(nq), pl.BlockSpec((1, 1, 2, H_A, HD_A, ATT_TILE), lambda b, i: (layer, b, 0, 0, 0, 0))],
        out_shape=[jax.ShapeDtypeStruct((nq, n, LANES), f32),
                   jax.ShapeDtypeStruct((depth, B, 2, H_A, HD_A, ATT_TILE), f32)],
        input_output_aliases=aliases,
        scratch_shapes=[pltpu.VMEM((2 * nq, ATT_TILE, LANES), f32), pltpu.VMEM((nq, ATT_TILE, LANES), f32),
                        pltpu.VMEM((nq, ATT_TILE, LANES), f32), pltpu.VMEM((nq, ATT_TILE, LANES), f32),
                        pltpu.VMEM((H_A, ATT_BLOCK, 2 * ATT_BLOCK), f32),
                        pltpu.VMEM((H_A, ATT_BLOCK, 2 * ATT_BLOCK), jnp.bfloat16),
                        pltpu.VMEM((H_A, ATT_BLOCK, LANES), f32), pltpu.VMEM((ATT_TILE, LANES), f32)],
        compiler_params=pltpu.CompilerParams(dimension_semantics=("arbitrary", "arbitrary"),
                                             vmem_limit_bytes=VMEM_LIMIT),
        name="pattn",
    )(*args)


def _sample_tables(wb, t_new):
    rows = np.arange(H_A * t_new)
    t = rows % t_new
    slope = np.asarray(_SLOPES)[rows // t_new]
    pos = np.arange(wb + t_new)
    d = wb + t[:, None] - pos[None, :]
    cnt = np.zeros(d.shape, np.float32)
    for win, dil in DILATED_PATTERNS:
        cnt += ((d >= 0) & (d % dil == 0) & (d <= win)).astype(np.float32)
    bias = np.where(cnt > 0, -slope[:, None] * d, -np.inf).astype(np.float32)
    return bias, cnt


def _outffn_kernel(x_ref, hm_ref, ha_ref, wo_ref, g1_ref, g2_ref, g3_ref, wg_ref, wu_ref, wd_ref, y_ref):
    ha = jnp.concatenate([ha_ref[j] for j in range(D_A // LANES)], axis=1)
    mix = _dot(_bf(hm_ref[...]), wo_ref[0:D_M, :]) + _dot(_bf(ha), wo_ref[D_M:D_M + D_A, :])
    x1 = x_ref[...] + _rms(mix, g1_ref[...])
    hf = _bf(_rms(x1, g2_ref[...]))
    gate = _dot(hf, wg_ref[...])
    up = _dot(hf, wu_ref[...])
    f = _dot(_bf(gate * jax.nn.sigmoid(gate) * up), wd_ref[...])
    y_ref[...] = x1 + _rms(f, g3_ref[...])


def _outffn(x, hm, ha, w_out, g_mix_post, g_ffn_pre, g_ffn_post, w_gate, w_up, w_down, tm):
    n, d = x.shape
    row = lambda w: pl.BlockSpec((tm, w), lambda i: (i, 0))
    return pl.pallas_call(
        _outffn_kernel,
        grid=(n // tm,),
        in_specs=[row(d), row(D_M), pl.BlockSpec((D_A // LANES, tm, LANES), lambda i: (0, i, 0)),
                  _resident(w_out.shape), _resident((1, d)), _resident((1, d)),
                  _resident((1, d)), _resident(w_gate.shape), _resident(w_up.shape), _resident(w_down.shape)],
        out_specs=row(d),
        out_shape=jax.ShapeDtypeStruct((n, d), jnp.float32),
        compiler_params=pltpu.CompilerParams(dimension_semantics=("arbitrary",), vmem_limit_bytes=VMEM_LIMIT),
        name="outffn",
    )(x, hm, ha, w_out, g_mix_post, g_ffn_pre, g_ffn_post, w_gate, w_up, w_down)


def _layer(xp, xs, lw, conv_s, c_all, n_s, m_s, cache, layer, kvt_all, c_all_new):
    B, S, D = xp.shape
    BS, T, _ = xs.shape
    assert T == SUBLANES and BS % (LANES // SUBLANES) == 0 and MAX_WINDOW == ATT_TILE
    f32 = jnp.float32
    hist = CONV_W - 1
    n_p, n_sm = B * S, BS * T
    xpf, xsf = xp.reshape(n_p, D), xs.reshape(n_sm, D)
    rows = lambda a, b, t: jnp.transpose(a, (1, 0, 2)).reshape(b, t, a.shape[0] * LANES)
    ffn = lambda x, hm, ha, tm: _outffn(x, hm, ha, lw["w_out"], lw["g_mix_post"], lw["g_ffn_pre"], lw["g_ffn_post"],
                                        lw["w_gate"], lw["w_up"], lw["w_down"], tm)
    qk_s, vm_s, om_s, qa_s, kva_s, gates_s = _inproj(xsf, lw["g_mix_pre"], lw["w_main"], lw["w_gates"],
                                                     _row_tile(n_sm, 1024))
    kv_rows = rows(kva_s, BS, T)
    qk, vm, om, qa, kva, gates, ha_s = _inproj(xpf, lw["g_mix_pre"], lw["w_main"], lw["w_gates"],
                                               _row_tile(S, HOST_ROWS),
                                               guest=(rows(qa_s, BS, T), kv_rows, cache, layer, 0, BS))

    r3 = lambda a: a.reshape(B, S, a.shape[-1])
    hm, c_p, n_p_new, m_p, conv_p = _mlstm(
        r3(qk), r3(vm), r3(om), r3(gates), lw["b_gates"], lw["w_conv"], lw["g_mh"],
        jnp.zeros((B, hist, 2 * D_M), f32), jnp.zeros((1, B, H_M, HD_M, HD_M), f32),
        jnp.zeros((B, H_M, 1, HD_M), f32), jnp.zeros((B, H_M, 1, LANES), f32), 0)
    ha, kvt_all = _prompt_attention(qa, kva, B, layer, c_all.shape[0], kvt_all)
    yp = ffn(xpf, hm.reshape(n_p, D_M), ha, _row_tile(n_p, 512))
    out_p = (kvt_all, c_p, n_p_new.reshape(B, H_M, HD_M), m_p[:, :, 0, 0], conv_p)

    conv_rows = jnp.pad(conv_s, ((0, 0), (0, T - hist), (0, 0))).reshape(n_sm, 2 * D_M)
    hm_s, c_s, n_s_new, m_s_new = _mlstm_short(
        qk_s, vm_s, om_s, gates_s, lw["b_gates"], lw["w_conv"], lw["g_mh"], conv_rows, c_all,
        jnp.transpose(n_s, (1, 0, 2)), jnp.pad(m_s, ((0, 0), (0, GATE_PAD - H_M))), layer, BS, c_all_new)
    ys = ffn(xsf, hm_s, jnp.transpose(ha_s.reshape(n_sm, D_A // LANES, LANES), (1, 0, 2)), _row_tile(n_sm, 512))
    out_s = (kv_rows.reshape(BS, T, 2, H_A, HD_A), c_s, jnp.transpose(n_s_new, (1, 0, 2)), m_s_new[:, :H_M],
             qk_s.reshape(BS, T, 2 * D_M)[:, T - hist:])
    return yp.reshape(B, S, D), ys.reshape(BS, T, D), out_p, out_s


def kernel(x_prompt, x_sample, cache_kv, state_C, state_n, state_m, state_conv, g_mix_pre, g_mix_post, g_ffn_pre,
           g_ffn_post, w_in, b_gates, w_conv, g_mh, w_out, w_gate, w_up, w_down):
    depth = w_in.shape[0]
    g0 = 4 * D_M
    g1 = g0 + 2 * H_M
    cache = jnp.transpose(cache_kv, (0, 1, 3, 4, 5, 2))
    w_in_t = jnp.transpose(w_in, (0, 2, 1))
    w_main_all = _bf(jnp.concatenate([w_in_t[:, :g0], w_in_t[:, g1:]], axis=1))
    w_gates_all = _bf(jnp.pad(w_in_t[:, g0:g1], ((0, 0), (0, GATE_PAD - 2 * H_M), (0, 0))))

    hp, hs = x_prompt, x_sample
    outs_p, outs_s = [], []
    kvt_all = jnp.zeros((depth, x_prompt.shape[0], 2, H_A, HD_A, ATT_TILE), jnp.float32)
    c_all_new = jnp.zeros(state_C.shape, jnp.float32)
    for l in range(depth):
        lw = {
            "g_mix_pre": g_mix_pre[l][None], "g_mix_post": g_mix_post[l][None],
            "g_ffn_pre": g_ffn_pre[l][None], "g_ffn_post": g_ffn_post[l][None],
            "w_main": w_main_all[l], "w_gates": w_gates_all[l],
            "b_gates": jnp.pad(b_gates[l], (0, GATE_PAD - 2 * H_M))[None],
            "w_conv": w_conv[l], "g_mh": g_mh[l][None],
            "w_out": _bf(w_out[l]), "w_gate": _bf(w_gate[l]), "w_up": _bf(w_up[l]), "w_down": _bf(w_down[l]),
        }
        hp, hs, out_p, out_s = _layer(hp, hs, lw, state_conv[l], state_C, state_n[l], state_m[l], cache, l,
                                      kvt_all, c_all_new)
        kvt_all, c_all_new = out_p[0], out_s[1]
        outs_p.append(out_p)
        outs_s.append(out_s)
    stack = lambda outs, i: jnp.stack([o[i] for o in outs])
    kv_prompt = jnp.transpose(kvt_all, (0, 1, 5, 2, 3, 4))
    return (hp, hs, kv_prompt, stack(outs_s, 0),
            stack(outs_p, 1), stack(outs_p, 2), stack(outs_p, 3), stack(outs_p, 4),
            c_all_new, stack(outs_s, 2), stack(outs_s, 3), stack(outs_s, 4))
```

```python
import functools

import numpy as np
import jax
import jax.numpy as jnp
from jax import lax
from jax.experimental import pallas as pl
from jax.experimental.pallas import tpu as pltpu

H_M = 5
HD_M = 128
D_M = H_M * HD_M
H_A = 6
HD_A = 64
D_A = H_A * HD_A
CONV_W = 4
MLSTM_MAX_CHUNK = 256
MLSTM_GROUP_TOKENS = 256
MLSTM_MAX_GROUP = 4
DILATED_PATTERNS = ((128, 1), (512, 4), (2048, 16))
MAX_WINDOW = 2048
EPS = 1e-6
LANES = 128
GATE_PAD = LANES
HOST_ROWS = 256
SUBLANES = 8
ATT_BLOCK = 128
ATT_STRIDE = max(d for _, d in DILATED_PATTERNS)
ATT_TILE = ATT_BLOCK * ATT_STRIDE
VMEM_LIMIT = 56 * 1024 * 1024

_SLOPES = [2.0 ** (-8.0 * (h + 1) / H_A) for h in range(H_A)]
_NEG_INF = float("-inf")


def _bf(x):
    return x.astype(jnp.bfloat16)


def _dot(a, b):
    return jnp.dot(a, b, preferred_element_type=jnp.float32)


def _dot_nt(a, b):
    return lax.dot_general(a, b, (((1,), (1,)), ((), ())), preferred_element_type=jnp.float32)


def _rms(x, g):
    return x * lax.rsqrt(jnp.mean(x * x, axis=-1, keepdims=True) + EPS) * g


def _resident(shape):
    nd = len(shape)
    return pl.BlockSpec(shape, lambda *_: (0,) * nd, pipeline_mode=pl.Buffered(1))


def _row_tile(n, cap):
    t = min(n, cap)
    while n % t:
        t //= 2
    return t


_IN_CUTS = (0, 2 * D_M, 3 * D_M, 4 * D_M, 4 * D_M + D_A, 4 * D_M + 3 * D_A)


_CONV_PAD = 8


def _sample_attention_rows(q_ref, kvn_ref, cache_ref, bias_ref, cnt_ref, biasn_ref, cntn_ref, o_ref, b):
    t_new = q_ref.shape[1]
    scale = HD_A ** -0.5
    heads = [(h * HD_A, (h + 1) * HD_A) for h in range(H_A)]
    qh = [_bf(q_ref[b, :, lo:hi]) for lo, hi in heads]
    sc = jnp.concatenate([_dot(qh[h], _bf(cache_ref[0, b, 0, h])) for h in range(H_A)], axis=0)
    sn = jnp.concatenate([_dot_nt(qh[h], _bf(kvn_ref[b, :, lo:hi])) for h, (lo, hi) in enumerate(heads)], axis=0)
    sc = sc * scale + bias_ref[...]
    sn = sn * scale + biasn_ref[...]
    mx = jnp.maximum(jnp.max(sc, axis=1, keepdims=True), jnp.max(sn, axis=1, keepdims=True))
    pc = cnt_ref[...] * jnp.exp(sc - mx)
    pn = cntn_ref[...] * jnp.exp(sn - mx)
    den = jnp.sum(pc, axis=1, keepdims=True) + jnp.sum(pn, axis=1, keepdims=True)
    outs = []
    for h, (lo, hi) in enumerate(heads):
        r0, r1 = h * t_new, (h + 1) * t_new
        o = (_dot_nt(_bf(pc[r0:r1]), _bf(cache_ref[0, b, 1, h]))
             + _dot(_bf(pn[r0:r1]), _bf(kvn_ref[b, :, D_A + lo:D_A + hi])))
        outs.append(o / den[r0:r1])
    o_ref[b] = jnp.concatenate(outs, axis=1)


def _inproj_kernel(*refs, tm, guests):
    if not guests:
        x_ref, g_ref, w_ref, wg_ref, qk_ref, vm_ref, om_ref, qa_ref, kva_ref, gates_ref = refs
    else:
        (x_ref, g_ref, w_ref, wg_ref, sq_ref, skv_ref, cache_ref, bias_ref, cnt_ref, biasn_ref, cntn_ref,
         qk_ref, vm_ref, om_ref, qa_ref, kva_ref, gates_ref, sha_ref, zs) = refs
    a = _bf(_rms(x_ref[...], g_ref[...]))
    nq = D_A // LANES
    att_tile = lambda j: (qa_ref, j) if j < nq else (kva_ref, j - nq)
    z = _dot_nt(a, w_ref[_IN_CUTS[3]:_IN_CUTS[5], :])
    if not guests:
        for j in range(3 * nq):
            dst, jj = att_tile(j)
            dst[jj] = z[:, j * LANES:(j + 1) * LANES]
    else:
        for j in range(3 * nq):
            zs[j] = z[:, j * LANES:(j + 1) * LANES]
    qk_ref[...] = _dot_nt(a, w_ref[_IN_CUTS[0]:_IN_CUTS[1], :])
    if guests:
        sub = pl.program_id(0) % (ATT_TILE // tm)
        per = tm // ATT_STRIDE
        for j in range(3 * nq):
            dst, jj = att_tile(j)
            for r in range(ATT_STRIDE):
                start = pl.multiple_of(r * ATT_BLOCK + sub * per, SUBLANES)
                dst[jj, pl.ds(start, per), :] = zs[j, pl.ds(r, per, stride=ATT_STRIDE), :]
    vm_ref[...] = _dot_nt(a, w_ref[_IN_CUTS[1]:_IN_CUTS[2], :])
    for b in range(guests):
        _sample_attention_rows(sq_ref, skv_ref, cache_ref, bias_ref, cnt_ref, biasn_ref, cntn_ref, sha_ref, b)
    om_ref[...] = _dot_nt(a, w_ref[_IN_CUTS[2]:_IN_CUTS[3], :])
    gates_ref[...] = _dot_nt(a, wg_ref[...])


def _guest_plumbing(guest, steps):
    sq, skv, cache_t, layer, first, count = guest
    T = sq.shape[1]
    wb = cache_t.shape[-1]
    assert count % steps == 0 and first % (count // steps) == 0
    per_step = count // steps
    off = first // per_step
    bias, cnt = _sample_tables(wb, T)
    consts = [bias[:, :wb], cnt[:, :wb], bias[:, wb:], cnt[:, wb:]]
    per_g = lambda w: pl.BlockSpec((per_step, T, w), lambda i: (off + i, 0, 0))
    in_specs = [per_g(D_A), per_g(2 * D_A),
                pl.BlockSpec((1, per_step, 2, H_A, HD_A, wb), lambda i: (layer, off + i, 0, 0, 0, 0))]
    in_specs += [_resident(c.shape) for c in consts]
    args = [sq, skv, cache_t] + [jnp.asarray(c) for c in consts]
    out_spec = pl.BlockSpec((per_step, T, D_A), lambda i: (i, 0, 0))
    return per_step, in_specs, args, out_spec, jax.ShapeDtypeStruct((count, T, D_A), jnp.float32)


def _inproj(x, g, w_main, w_gates, tm, guest=None):
    n, d = x.shape
    row = lambda w: pl.BlockSpec((tm, w), lambda i: (i, 0))
    f32 = jnp.float32
    in_specs = [row(d), _resident((1, d)), _resident(w_main.shape), _resident(w_gates.shape)]
    out_shape = [jax.ShapeDtypeStruct((n, 2 * D_M), f32), jax.ShapeDtypeStruct((n, D_M), f32),
                 jax.ShapeDtypeStruct((n, D_M), f32), jax.ShapeDtypeStruct((D_A // LANES, n, LANES), f32),
                 jax.ShapeDtypeStruct((2 * D_A // LANES, n, LANES), f32), jax.ShapeDtypeStruct((n, GATE_PAD), f32)]
    args = [x, g, w_main, w_gates]
    scratch = []
    guests = 0
    if guest is None:
        tiles = lambda w: pl.BlockSpec((w // LANES, tm, LANES), lambda i: (0, i, 0))
        out_specs = [row(2 * D_M), row(D_M), row(D_M), tiles(D_A), tiles(2 * D_A), row(GATE_PAD)]
    else:
        assert n % ATT_TILE == 0 and ATT_TILE % tm == 0 and tm % (ATT_STRIDE * SUBLANES) == 0
        guests, g_specs, g_args, g_out_spec, g_out_shape = _guest_plumbing(guest, n // tm)
        per_att = ATT_TILE // tm
        tiles = lambda w: pl.BlockSpec((w // LANES, ATT_TILE, LANES), lambda i: (0, i // per_att, 0),
                                       pipeline_mode=pl.Buffered(1))
        in_specs += g_specs
        out_specs = [row(2 * D_M), row(D_M), row(D_M), tiles(D_A), tiles(2 * D_A), row(GATE_PAD), g_out_spec]
        out_shape.append(g_out_shape)
        args += g_args
        scratch = [pltpu.VMEM((3 * D_A // LANES, tm, LANES), f32)]
    return pl.pallas_call(
        functools.partial(_inproj_kernel, tm=tm, guests=guests),
        grid=(n // tm,),
        in_specs=in_specs,
        out_specs=out_specs,
        out_shape=out_shape,
        scratch_shapes=scratch,
        compiler_params=pltpu.CompilerParams(dimension_semantics=("arbitrary",), vmem_limit_bytes=VMEM_LIMIT),
        name="inproj",
    )(*args)


def _mlstm_head(q, k, v, a_col, b, a_row, m, cmat, nvec, causal, eye):
    L = q.shape[0]
    qb, kb, vb = _bf(q), _bf(k), _bf(v)
    if a_row is None:
        a_row = jnp.sum(jnp.where(eye, a_col, 0.0), axis=0, keepdims=True)
    dlog = jnp.where(causal, b + a_row, _NEG_INF)
    inter = b + m
    mt = jnp.maximum(inter, jnp.max(dlog, axis=1, keepdims=True))
    s = _dot_nt(qb, kb) * jnp.exp(dlog - mt)
    e_inter = jnp.exp(inter - mt)
    num = _dot(_bf(s), vb) + e_inter * _dot_nt(qb, _bf(cmat))
    den = jnp.sum(s, axis=1, keepdims=True) + e_inter * jnp.sum(q * nvec, axis=1, keepdims=True)
    hh = num / jnp.maximum(jnp.abs(den), jnp.exp(-mt))

    b_last = b[L - 1:L, :]
    wlog = b_last + a_col
    m_new = jnp.maximum(b_last + m, jnp.max(wlog, axis=0, keepdims=True))
    w = jnp.exp(wlog - m_new)
    decay = jnp.exp(b_last + m - m_new)
    wv = w * v
    if L < HD_M:
        pad = jnp.zeros((HD_M - L, HD_M), jnp.float32)
        wv_t = _bf(jnp.concatenate([wv, pad], axis=0).T)
        k_pad = _bf(jnp.concatenate([k, pad], axis=0))
    else:
        wv_t = _bf(wv.T)
        k_pad = kb
    c_new = decay * cmat + _dot(wv_t, k_pad)
    n_new = decay * nvec + jnp.sum(w * k, axis=0, keepdims=True)
    return hh, c_new, n_new, m_new


def _mlstm_kernel(qk_ref, v_ref, o_ref, g_ref, bias_ref, wconv_ref, gmh_ref, conv0_ref, c0_ref, n0_ref, m0_ref,
                  h_ref, c_out, n_out, m_out, conv_out, ubuf, c_s, n_s, m_s, *, chunk, group):
    L = chunk
    c = pl.program_id(1)
    hist = CONV_W - 1

    @pl.when(c == 0)
    def _():
        ubuf[:, _CONV_PAD - hist:_CONV_PAD, :] = conv0_ref[...]
        c_s[...] = c0_ref[0]
        n_s[...] = n0_ref[...]
        m_s[...] = m0_ref[...]

    row_id = lax.broadcasted_iota(jnp.int32, (L, GATE_PAD), 0)
    ti = lax.broadcasted_iota(jnp.int32, (L, L), 0)
    si = lax.broadcasted_iota(jnp.int32, (L, L), 1)
    causal = si <= ti
    eye = si == ti

    for g in range(group):
        ubuf[g, _CONV_PAD:_CONV_PAD + L, :] = qk_ref[g]
        conv = ubuf[g, _CONV_PAD - hist:_CONV_PAD - hist + L, :] * wconv_ref[0:1, :]
        for i in range(1, CONV_W):
            conv = conv + ubuf[g, _CONV_PAD - hist + i:_CONV_PAD - hist + i + L, :] * wconv_ref[i:i + 1, :]
        tail = ubuf[g, _CONV_PAD + L - hist:_CONV_PAD + L, :]
        ubuf[g, _CONV_PAD - hist:_CONV_PAD, :] = tail
        conv_out[g] = tail
        qk = conv * jax.nn.sigmoid(conv)

        gates = g_ref[g] + bias_ref[...]
        lf = jnp.minimum(gates, 0.0) - jnp.log(1.0 + jnp.exp(-jnp.abs(gates)))
        bcum = lf
        step = 1
        while step < L:
            bcum = bcum + jnp.where(row_id >= step, pltpu.roll(bcum, step, 0), 0.0)
            step *= 2

        a_all = gates - pltpu.roll(bcum, GATE_PAD - H_M, 1)
        a_rows = a_all.T if L % LANES == 0 else None

        for h in range(H_M):
            lanes = slice(h * HD_M, (h + 1) * HD_M)
            hh, c_new, n_new, m_new = _mlstm_head(
                qk[:, lanes], qk[:, D_M + h * HD_M:D_M + (h + 1) * HD_M] * (HD_M ** -0.5), v_ref[g, :, lanes],
                a_all[:, h:h + 1], bcum[:, H_M + h:H_M + h + 1], None if a_rows is None else a_rows[h:h + 1, :],
                m_s[g, h][:, 0:1], c_s[g, h], n_s[g, h], causal, eye)
            c_s[g, h] = c_new
            n_s[g, h] = n_new
            m_s[g, h] = jnp.broadcast_to(m_new, (1, LANES))
            h_ref[g, :, lanes] = jax.nn.sigmoid(o_ref[g, :, lanes]) * _rms(hh, gmh_ref[:, lanes])

    @pl.when(c == pl.num_programs(1) - 1)
    def _():
        c_out[...] = c_s[...]
        n_out[...] = n_s[...]
        m_out[...] = m_s[...]


def _mlstm_tiling(B, T):
    chunk = _row_tile(T, MLSTM_MAX_CHUNK)
    group = _row_tile(B, max(1, min(MLSTM_MAX_GROUP, MLSTM_GROUP_TOKENS // chunk)))
    return chunk, group


def _mlstm(qk, vm, om, gates, b_gates, w_conv, g_mh, conv0, c0, n0, m0, layer):
    B, T, _ = qk.shape
    chunk, G = _mlstm_tiling(B, T)
    tok = lambda w: pl.BlockSpec((G, chunk, w), lambda b, c: (b, c, 0))
    per_b = lambda shape: pl.BlockSpec((G,) + shape, lambda b, c: (b,) + (0,) * len(shape))
    hist = CONV_W - 1
    f32 = jnp.float32
    return pl.pallas_call(
        functools.partial(_mlstm_kernel, chunk=chunk, group=G),
        grid=(B // G, T // chunk),
        in_specs=[tok(2 * D_M), tok(D_M), tok(D_M), tok(GATE_PAD),
                  _resident((1, GATE_PAD)), _resident((CONV_W, 2 * D_M)), _resident((1, D_M)),
                  per_b((hist, 2 * D_M)),
                  pl.BlockSpec((1, G, H_M, HD_M, HD_M), lambda b, c: (layer, b, 0, 0, 0)),
                  per_b((H_M, 1, HD_M)), per_b((H_M, 1, LANES))],
        out_specs=[tok(D_M), per_b((H_M, HD_M, HD_M)), per_b((H_M, 1, HD_M)), per_b((H_M, 1, LANES)),
                   per_b((hist, 2 * D_M))],
        out_shape=[jax.ShapeDtypeStruct((B, T, D_M), f32),
                   jax.ShapeDtypeStruct((B, H_M, HD_M, HD_M), f32),
                   jax.ShapeDtypeStruct((B, H_M, 1, HD_M), f32),
                   jax.ShapeDtypeStruct((B, H_M, 1, LANES), f32),
                   jax.ShapeDtypeStruct((B, hist, 2 * D_M), f32)],
        scratch_shapes=[pltpu.VMEM((G, _CONV_PAD + chunk, 2 * D_M), f32),
                        pltpu.VMEM((G, H_M, HD_M, HD_M), f32),
                        pltpu.VMEM((G, H_M, 1, HD_M), f32),
                        pltpu.VMEM((G, H_M, 1, LANES), f32)],
        compiler_params=pltpu.CompilerParams(dimension_semantics=("arbitrary", "arbitrary"),
                                             vmem_limit_bytes=VMEM_LIMIT),
        name="mlstm",
    )(qk, vm, om, gates, b_gates, w_conv, g_mh, conv0, c0, n0, m0)


def _mlstm_short_kernel(qk_ref, v_ref, o_ref, g_ref, hist_ref, bias_ref, wconv_ref, gmh_ref, c0_ref, n0_ref, m0_ref,
                        c_alias_ref, h_ref, c_out, n_out, m_out, qx, kx, *, group):
    G, T = group, SUBLANES
    R = G * T
    f32 = jnp.float32

    @pl.when(pl.program_id(0) == 0)
    def _():
        qx[...] = jnp.zeros(qx.shape, f32)
        kx[...] = jnp.zeros(kx.shape, f32)

    grp = lambda x: x.reshape(G, T, x.shape[-1])
    rows_of = lambda x: jnp.broadcast_to(x[:, None, :], (G, T, x.shape[-1])).reshape(R, x.shape[-1])
    shift_gate = lambda x: pltpu.roll(x, GATE_PAD - H_M, 1)

    t_wide = lax.broadcasted_iota(jnp.int32, (R, 2 * D_M), 0) % T
    u = qk_ref[...]
    hist = hist_ref[...]
    conv = u * wconv_ref[CONV_W - 1:CONV_W, :]
    for k in range(1, CONV_W):
        back = jnp.where(t_wide >= k, pltpu.roll(u, k, 0), pltpu.roll(hist, (R - (CONV_W - 1 - k)) % R, 0))
        conv = conv + back * wconv_ref[CONV_W - 1 - k:CONV_W - k, :]
    qk = conv * jax.nn.sigmoid(conv)

    gates = g_ref[...] + bias_ref[...]
    lf = jnp.minimum(gates, 0.0) - jnp.log(1.0 + jnp.exp(-jnp.abs(gates)))
    t_id = lax.broadcasted_iota(jnp.int32, (R, GATE_PAD), 0) % T
    bcum = lf
    step = 1
    while step < T:
        bcum = bcum + jnp.where(t_id >= step, pltpu.roll(bcum, step, 0), 0.0)
        step *= 2

    a_all = gates - shift_gate(bcum)
    a_rows = a_all.T
    b_last = shift_gate(rows_of(grp(bcum)[:, T - 1, :]))
    m_rows = rows_of(m0_ref[...])
    wlog = b_last + a_all
    m_new = jnp.maximum(b_last + m_rows, rows_of(jnp.max(grp(wlog), axis=1)))
    w_all = jnp.exp(wlog - m_new)
    decay = grp(jnp.exp(b_last + m_rows - m_new))[:, 0, :]
    m_out[...] = grp(m_new)[:, 0, :]

    ri = lax.broadcasted_iota(jnp.int32, (R, R), 0)
    ci = lax.broadcasted_iota(jnp.int32, (R, R), 1)
    mask = (ri // T == ci // T) & (ci <= ri)

    for h in range(H_M):
        lanes = slice(h * HD_M, (h + 1) * HD_M)
        q = qk[:, lanes]
        k = qk[:, D_M + h * HD_M:D_M + (h + 1) * HD_M] * (HD_M ** -0.5)
        v = v_ref[:, lanes]
        qb, kb, vb = _bf(q), _bf(k), _bf(v)
        b = bcum[:, H_M + h:H_M + h + 1]
        m = m_rows[:, h:h + 1]
        dlog = jnp.where(mask, b + a_rows[h:h + 1, :], _NEG_INF)
        inter = b + m
        mt = jnp.maximum(inter, jnp.max(dlog, axis=1, keepdims=True))
        s = _dot_nt(qb, kb) * jnp.exp(dlog - mt)
        e_inter = jnp.exp(inter - mt)
        for g in range(G):
            qx[g * T:(g + 1) * T, g * HD_M:(g + 1) * HD_M] = q[g * T:(g + 1) * T, :]
            kx[g * T:(g + 1) * T, g * HD_M:(g + 1) * HD_M] = k[g * T:(g + 1) * T, :]
        c_cat = jnp.concatenate([c0_ref[0, g, h] for g in range(G)], axis=1)
        n_h = n0_ref[h]
        num = _dot(_bf(s), vb) + e_inter * _dot_nt(_bf(qx[...]), _bf(c_cat))
        den = jnp.sum(s, axis=1, keepdims=True) + e_inter * jnp.sum(q * rows_of(n_h), axis=1, keepdims=True)
        hh = num / jnp.maximum(jnp.abs(den), jnp.exp(-mt))
        h_ref[:, lanes] = jax.nn.sigmoid(o_ref[:, lanes]) * _rms(hh, gmh_ref[:, lanes])

        w = w_all[:, h:h + 1]
        upd = _dot(_bf((w * v).T), _bf(kx[...]))
        for g in range(G):
            c_out[0, g, h] = decay[g:g + 1, h:h + 1] * c0_ref[0, g, h] + upd[:, g * HD_M:(g + 1) * HD_M]
        n_out[h] = decay[:, h:h + 1] * n_h + jnp.sum(grp(w * k), axis=1)


def _mlstm_short(qk, vm, om, gates, b_gates, w_conv, g_mh, hist, c0, n0, m0, layer, B, c_all_new):
    T = SUBLANES
    G = LANES // T
    assert B % G == 0 and qk.shape[0] == B * T
    R = G * T
    row = lambda w: pl.BlockSpec((R, w), lambda i: (i, 0))
    f32 = jnp.float32
    in_specs = [row(2 * D_M), row(D_M), row(D_M), row(GATE_PAD), row(2 * D_M),
                _resident((1, GATE_PAD)), _resident((CONV_W, 2 * D_M)), _resident((1, D_M)),
                pl.BlockSpec((1, G, H_M, HD_M, HD_M), lambda i: (layer, i, 0, 0, 0)),
                pl.BlockSpec((H_M, G, HD_M), lambda i: (0, i, 0)),
                pl.BlockSpec((G, GATE_PAD), lambda i: (i, 0))]
    args = [qk, vm, om, gates, hist, b_gates, w_conv, g_mh, c0, n0, m0, c_all_new]
    aliases = {len(args) - 1: 1}
    in_specs.append(pl.BlockSpec(memory_space=pl.ANY))
    return pl.pallas_call(
        functools.partial(_mlstm_short_kernel, group=G),
        grid=(B // G,),
        in_specs=in_specs,
        out_specs=[row(D_M), pl.BlockSpec((1, G, H_M, HD_M, HD_M), lambda i: (layer, i, 0, 0, 0)),
                   pl.BlockSpec((H_M, G, HD_M), lambda i: (0, i, 0)), pl.BlockSpec((G, GATE_PAD), lambda i: (i, 0))],
        out_shape=[jax.ShapeDtypeStruct((B * T, D_M), f32), jax.ShapeDtypeStruct(c0.shape, f32),
                   jax.ShapeDtypeStruct((H_M, B, HD_M), f32), jax.ShapeDtypeStruct((B, GATE_PAD), f32)],
        input_output_aliases=aliases,
        scratch_shapes=[pltpu.VMEM((R, G * HD_M), f32), pltpu.VMEM((R, G * HD_M), f32)],
        compiler_params=pltpu.CompilerParams(dimension_semantics=("arbitrary",), vmem_limit_bytes=VMEM_LIMIT),
        name="mlstm_short",
    )(*args)


def _pattn_kernel(q_ref, kv_ref, kvt_alias_ref, o_ref, kvt_ref, prev_s, acc_s, m_s, l_s, s_buf, p_buf, m_buf, tok_s):
    i = pl.program_id(1)
    nb = ATT_BLOCK
    pair = 2 * HD_A
    scale = HD_A ** -0.5

    @pl.when(i == 0)
    def _():
        prev_s[...] = jnp.zeros(prev_s.shape, prev_s.dtype)

    pq = lax.broadcasted_iota(jnp.int32, (nb, nb), 0)
    pk = lax.broadcasted_iota(jnp.int32, (nb, nb), 1)
    lo_half = lax.broadcasted_iota(jnp.int32, (nb, pair), 1) < HD_A
    ones = jnp.ones((2 * nb, pair), jnp.bfloat16)
    no_keys = jnp.full((nb, nb), _NEG_INF, jnp.float32)
    npat = len(DILATED_PATTERNS)
    npair = H_A // 2

    for pi, (win, dil) in enumerate(DILATED_PATTERNS):
        first, last = pi == 0, pi == npat - 1
        nrun = ATT_STRIDE // dil
        rlen = nb // nrun
        nblk = nrun
        dist = nrun * (pq % rlen) + pq // rlen - (nrun * (pk % rlen) + pk // rlen)
        nd_cur = jnp.where(dist >= 0, -(dist * dil).astype(jnp.float32), _NEG_INF)
        nd_prev = jnp.where(dist <= 0, -((dist + nb) * dil).astype(jnp.float32), _NEG_INF)
        nd_prev_first = jnp.where(i > 0, nd_prev, no_keys)

        def starts(c, jb, dil=dil, nrun=nrun, rlen=rlen):
            rows = [(dil * e + c) * nb + rlen * jb for e in range(nrun)]
            return [r if isinstance(r, int) else pl.multiple_of(r, SUBLANES) for r in rows]

        def load(ref, j, st, rlen=rlen):
            return jnp.concatenate([ref[j, pl.ds(s, rlen), :] for s in st], axis=0)

        def store(ref, j, st, val, rlen=rlen):
            for e, s in enumerate(st):
                ref[j, pl.ds(s, rlen), :] = val[e * rlen:(e + 1) * rlen]

        def unit(c, jb, prev_ref, prev_jb, nd_prev_u, slot, first=first, last=last, nd_cur=nd_cur, starts=starts,
                 load=load, store=store):
            sb = slot * H_A
            rows = starts(c, jb)
            prows = starts(c, prev_jb)
            nd = jnp.concatenate([nd_cur, nd_prev_u], axis=1)
            for hp in range(npair):
                q2 = load(q_ref, hp, rows) * scale
                k2 = _bf(jnp.concatenate([load(kv_ref, hp, rows), load(prev_ref, hp, prows)], axis=0))
                for e in range(2):
                    qm = _bf(jnp.where(lo_half if e == 0 else jnp.logical_not(lo_half), q2, 0.0))
                    s_buf[sb + 2 * hp + e] = _dot_nt(qm, k2) + _SLOPES[2 * hp + e] * nd
            for h in range(H_A):
                s = s_buf[sb + h]
                m = jnp.max(s, axis=1, keepdims=True)
                p_buf[sb + h] = _bf(jnp.exp(s - m))
                m_buf[sb + h] = jnp.broadcast_to(m, (nb, pair))
            for hp in range(npair):
                v2 = _bf(jnp.concatenate([load(kv_ref, npair + hp, rows), load(prev_ref, npair + hp, prows)], axis=0))
                rhs = jnp.concatenate([v2, ones], axis=1)
                r_e = _dot(p_buf[sb + 2 * hp], rhs)
                r_o = _dot(p_buf[sb + 2 * hp + 1], rhs)
                acc2 = jnp.where(lo_half, r_e[:, :pair], r_o[:, :pair])
                l2 = jnp.where(lo_half, r_e[:, pair:], r_o[:, pair:])
                m2 = jnp.where(lo_half, m_buf[sb + 2 * hp], m_buf[sb + 2 * hp + 1])
                if not first:
                    m_old = load(m_s, hp, rows)
                    m_new = jnp.maximum(m_old, m2)
                    a_old = jnp.exp(m_old - m_new)
                    a_loc = jnp.exp(m2 - m_new)
                    acc2 = a_old * load(acc_s, hp, rows) + a_loc * acc2
                    l2 = a_old * load(l_s, hp, rows) + a_loc * l2
                    m2 = m_new
                if last:
                    o_ref[hp, pl.ds(c, nb, stride=ATT_STRIDE), :] = acc2 / l2
                else:
                    store(acc_s, hp, rows, acc2)
                    store(m_s, hp, rows, m2)
                    store(l_s, hp, rows, l2)

        def first_block(c, slot, nblk=nblk, unit=unit, nd_prev_first=nd_prev_first):
            unit(c, 0, prev_s, nblk - 1, nd_prev_first, slot)

        def later_block(u, slot, dil=dil, unit=unit, nd_prev=nd_prev):
            jb = 1 + u // dil
            unit(u % dil, jb, kv_ref, jb - 1, nd_prev, slot)

        def run(count, block):
            def two(p, carry):
                block(2 * p, 0)
                block(2 * p + 1, 1)
                return carry
            if count // 2:
                lax.fori_loop(0, count // 2, two, 0)
            if count % 2:
                block(count - 1, 0)

        run(dil, first_block)
        if nblk > 1:
            run(dil * (nblk - 1), later_block)

    prev_s[...] = kv_ref[...]

    @pl.when(i == pl.num_programs(1) - 1)
    def _():
        for j in range(2 * npair):
            for r in range(ATT_STRIDE):
                tok_s[pl.ds(r, nb, stride=ATT_STRIDE), :] = kv_ref[j, r * nb:(r + 1) * nb, :]
            t = tok_s[...].T
            kvt_ref[0, 0, j // npair, 2 * (j % npair)] = t[0:HD_A]
            kvt_ref[0, 0, j // npair, 2 * (j % npair) + 1] = t[HD_A:pair]


def _prompt_attention(q, kv, B, layer, depth, kvt_all):
    nq, n, _ = q.shape
    S = n // B
    for win, dil in DILATED_PATTERNS:
        assert win // dil == ATT_BLOCK and ATT_TILE % (dil * ATT_BLOCK) == 0
    assert S % ATT_TILE == 0 and LANES == 2 * HD_A
    assert DILATED_PATTERNS[-1][1] == ATT_STRIDE and all(ATT_STRIDE % d == 0 for _, d in DILATED_PATTERNS)
    nt = S // ATT_TILE
    tile = lambda k: pl.BlockSpec((k, ATT_TILE, LANES), lambda b, i: (0, b * nt + i, 0))
    f32 = jnp.float32
    in_specs = [tile(nq), tile(2 * nq), pl.BlockSpec(memory_space=pl.ANY)]
    args = [q, kv, kvt_all]
    aliases = {2: 1}
    return pl.pallas_call(
        _pattn_kernel,
        grid=(B, nt),
        in_specs=in_specs,
        out_specs=[tile(nq), pl.BlockSpec((1, 1, 2, H_A, HD_A, ATT_TILE), lambda b, i: (layer, b, 0, 0, 0, 0))],
        out_shape=[jax.ShapeDtypeStruct((nq, n, LANES), f32),
                   jax.ShapeDtypeStruct((depth, B, 2, H_A, HD_A, ATT_TILE), f32)],
        input_output_aliases=aliases,
        scratch_shapes=[pltpu.VMEM((2 * nq, ATT_TILE, LANES), f32), pltpu.VMEM((nq, ATT_TILE, LANES), f32),
                        pltpu.VMEM((nq, ATT_TILE, LANES), f32), pltpu.VMEM((nq, ATT_TILE, LANES), f32),
                        pltpu.VMEM((2 * H_A, ATT_BLOCK, 2 * ATT_BLOCK), f32),
                        pltpu.VMEM((2 * H_A, ATT_BLOCK, 2 * ATT_BLOCK), jnp.bfloat16),
                        pltpu.VMEM((2 * H_A, ATT_BLOCK, LANES), f32), pltpu.VMEM((ATT_TILE, LANES), f32)],
        compiler_params=pltpu.CompilerParams(dimension_semantics=("arbitrary", "arbitrary"),
                                             vmem_limit_bytes=VMEM_LIMIT),
        name="pattn",
    )(*args)


def _sample_tables(wb, t_new):
    rows = np.arange(H_A * t_new)
    t = rows % t_new
    slope = np.asarray(_SLOPES)[rows // t_new]
    pos = np.arange(wb + t_new)
    d = wb + t[:, None] - pos[None, :]
    cnt = np.zeros(d.shape, np.float32)
    for win, dil in DILATED_PATTERNS:
        cnt += ((d >= 0) & (d % dil == 0) & (d <= win)).astype(np.float32)
    bias = np.where(cnt > 0, -slope[:, None] * d, -np.inf).astype(np.float32)
    return bias, cnt


def _outffn_kernel(x_ref, hm_ref, ha_ref, wo_ref, g1_ref, g2_ref, g3_ref, wg_ref, wu_ref, wd_ref, y_ref):
    ha = jnp.concatenate([ha_ref[j] for j in range(D_A // LANES)], axis=1)
    mix = _dot(_bf(hm_ref[...]), wo_ref[0:D_M, :]) + _dot(_bf(ha), wo_ref[D_M:D_M + D_A, :])
    x1 = x_ref[...] + _rms(mix, g1_ref[...])
    hf = _bf(_rms(x1, g2_ref[...]))
    gate = _dot(hf, wg_ref[...])
    up = _dot(hf, wu_ref[...])
    f = _dot(_bf(gate * jax.nn.sigmoid(gate) * up), wd_ref[...])
    y_ref[...] = x1 + _rms(f, g3_ref[...])


def _outffn(x, hm, ha, w_out, g_mix_post, g_ffn_pre, g_ffn_post, w_gate, w_up, w_down, tm):
    n, d = x.shape
    row = lambda w: pl.BlockSpec((tm, w), lambda i: (i, 0))
    return pl.pallas_call(
        _outffn_kernel,
        grid=(n // tm,),
        in_specs=[row(d), row(D_M), pl.BlockSpec((D_A // LANES, tm, LANES), lambda i: (0, i, 0)),
                  _resident(w_out.shape), _resident((1, d)), _resident((1, d)),
                  _resident((1, d)), _resident(w_gate.shape), _resident(w_up.shape), _resident(w_down.shape)],
        out_specs=row(d),
        out_shape=jax.ShapeDtypeStruct((n, d), jnp.float32),
        compiler_params=pltpu.CompilerParams(dimension_semantics=("arbitrary",), vmem_limit_bytes=VMEM_LIMIT),
        name="outffn",
    )(x, hm, ha, w_out, g_mix_post, g_ffn_pre, g_ffn_post, w_gate, w_up, w_down)


def _layer(xp, xs, lw, conv_s, c_all, n_s, m_s, cache, layer, kvt_all, c_all_new):
    B, S, D = xp.shape
    BS, T, _ = xs.shape
    assert T == SUBLANES and BS % (LANES // SUBLANES) == 0 and MAX_WINDOW == ATT_TILE
    f32 = jnp.float32
    hist = CONV_W - 1
    n_p, n_sm = B * S, BS * T
    xpf, xsf = xp.reshape(n_p, D), xs.reshape(n_sm, D)
    rows = lambda a, b, t: jnp.transpose(a, (1, 0, 2)).reshape(b, t, a.shape[0] * LANES)
    ffn = lambda x, hm, ha, tm: _outffn(x, hm, ha, lw["w_out"], lw["g_mix_post"], lw["g_ffn_pre"], lw["g_ffn_post"],
                                        lw["w_gate"], lw["w_up"], lw["w_down"], tm)
    qk_s, vm_s, om_s, qa_s, kva_s, gates_s = _inproj(xsf, lw["g_mix_pre"], lw["w_main"], lw["w_gates"],
                                                     _row_tile(n_sm, 1024))
    kv_rows = rows(kva_s, BS, T)
    qk, vm, om, qa, kva, gates, ha_s = _inproj(xpf, lw["g_mix_pre"], lw["w_main"], lw["w_gates"],
                                               _row_tile(S, HOST_ROWS),
                                               guest=(rows(qa_s, BS, T), kv_rows, cache, layer, 0, BS))

    r3 = lambda a: a.reshape(B, S, a.shape[-1])
    hm, c_p, n_p_new, m_p, conv_p = _mlstm(
        r3(qk), r3(vm), r3(om), r3(gates), lw["b_gates"], lw["w_conv"], lw["g_mh"],
        jnp.zeros((B, hist, 2 * D_M), f32), jnp.zeros((1, B, H_M, HD_M, HD_M), f32),
        jnp.zeros((B, H_M, 1, HD_M), f32), jnp.zeros((B, H_M, 1, LANES), f32), 0)
    ha, kvt_all = _prompt_attention(qa, kva, B, layer, c_all.shape[0], kvt_all)
    yp = ffn(xpf, hm.reshape(n_p, D_M), ha, _row_tile(n_p, 512))
    out_p = (kvt_all, c_p, n_p_new.reshape(B, H_M, HD_M), m_p[:, :, 0, 0], conv_p)

    conv_rows = jnp.pad(conv_s, ((0, 0), (0, T - hist), (0, 0))).reshape(n_sm, 2 * D_M)
    hm_s, c_s, n_s_new, m_s_new = _mlstm_short(
        qk_s, vm_s, om_s, gates_s, lw["b_gates"], lw["w_conv"], lw["g_mh"], conv_rows, c_all,
        jnp.transpose(n_s, (1, 0, 2)), jnp.pad(m_s, ((0, 0), (0, GATE_PAD - H_M))), layer, BS, c_all_new)
    ys = ffn(xsf, hm_s, jnp.transpose(ha_s.reshape(n_sm, D_A // LANES, LANES), (1, 0, 2)), _row_tile(n_sm, 512))
    out_s = (kv_rows.reshape(BS, T, 2, H_A, HD_A), c_s, jnp.transpose(n_s_new, (1, 0, 2)), m_s_new[:, :H_M],
             qk_s.reshape(BS, T, 2 * D_M)[:, T - hist:])
    return yp.reshape(B, S, D), ys.reshape(BS, T, D), out_p, out_s


def kernel(x_prompt, x_sample, cache_kv, state_C, state_n, state_m, state_conv, g_mix_pre, g_mix_post, g_ffn_pre,
           g_ffn_post, w_in, b_gates, w_conv, g_mh, w_out, w_gate, w_up, w_down):
    depth = w_in.shape[0]
    g0 = 4 * D_M
    g1 = g0 + 2 * H_M
    cache = jnp.transpose(cache_kv, (0, 1, 3, 4, 5, 2))
    w_in_t = jnp.transpose(w_in, (0, 2, 1))
    w_main_all = _bf(jnp.concatenate([w_in_t[:, :g0], w_in_t[:, g1:]], axis=1))
    w_gates_all = _bf(jnp.pad(w_in_t[:, g0:g1], ((0, 0), (0, GATE_PAD - 2 * H_M), (0, 0))))

    hp, hs = x_prompt, x_sample
    outs_p, outs_s = [], []
    kvt_all = jnp.zeros((depth, x_prompt.shape[0], 2, H_A, HD_A, ATT_TILE), jnp.float32)
    c_all_new = jnp.zeros(state_C.shape, jnp.float32)
    for l in range(depth):
        lw = {
            "g_mix_pre": g_mix_pre[l][None], "g_mix_post": g_mix_post[l][None],
            "g_ffn_pre": g_ffn_pre[l][None], "g_ffn_post": g_ffn_post[l][None],
            "w_main": w_main_all[l], "w_gates": w_gates_all[l],
            "b_gates": jnp.pad(b_gates[l], (0, GATE_PAD - 2 * H_M))[None],
            "w_conv": w_conv[l], "g_mh": g_mh[l][None],
            "w_out": _bf(w_out[l]), "w_gate": _bf(w_gate[l]), "w_up": _bf(w_up[l]), "w_down": _bf(w_down[l]),
        }
        hp, hs, out_p, out_s = _layer(hp, hs, lw, state_conv[l], state_C, state_n[l], state_m[l], cache, l,
                                      kvt_all, c_all_new)
        kvt_all, c_all_new = out_p[0], out_s[1]
        outs_p.append(out_p)
        outs_s.append(out_s)
    stack = lambda outs, i: jnp.stack([o[i] for o in outs])
    kv_prompt = jnp.transpose(kvt_all, (0, 1, 5, 2, 3, 4))
    return (hp, hs, kv_prompt, stack(outs_s, 0),
            stack(outs_p, 1), stack(outs_p, 2), stack(outs_p, 3), stack(outs_p, 4),
            c_all_new, stack(outs_s, 2), stack(outs_s, 3), stack(outs_s, 4))
```

```python
import functools

import numpy as np
import jax
import jax.numpy as jnp
from jax import lax
from jax.experimental import pallas as pl
from jax.experimental.pallas import tpu as pltpu

H_M = 5
HD_M = 128
D_M = H_M * HD_M
H_A = 6
HD_A = 64
D_A = H_A * HD_A
CONV_W = 4
MLSTM_MAX_CHUNK = 256
MLSTM_GROUP_TOKENS = 256
MLSTM_MAX_GROUP = 4
DILATED_PATTERNS = ((128, 1), (512, 4), (2048, 16))
MAX_WINDOW = 2048
EPS = 1e-6
LANES = 128
GATE_PAD = LANES
HOST_ROWS = 256
SUBLANES = 8
ATT_BLOCK = 128
ATT_STRIDE = max(d for _, d in DILATED_PATTERNS)
ATT_UNROLL = 3
ATT_TILE = ATT_BLOCK * ATT_STRIDE
VMEM_LIMIT = 56 * 1024 * 1024

_SLOPES = [2.0 ** (-8.0 * (h + 1) / H_A) for h in range(H_A)]
_NEG_INF = float("-inf")


def _bf(x):
    return x.astype(jnp.bfloat16)


def _dot(a, b):
    return jnp.dot(a, b, preferred_element_type=jnp.float32)


def _dot_nt(a, b):
    return lax.dot_general(a, b, (((1,), (1,)), ((), ())), preferred_element_type=jnp.float32)


def _rms(x, g):
    return x * lax.rsqrt(jnp.mean(x * x, axis=-1, keepdims=True) + EPS) * g


def _resident(shape):
    nd = len(shape)
    return pl.BlockSpec(shape, lambda *_: (0,) * nd, pipeline_mode=pl.Buffered(1))


def _row_tile(n, cap):
    t = min(n, cap)
    while n % t:
        t //= 2
    return t


_IN_CUTS = (0, 2 * D_M, 3 * D_M, 4 * D_M, 4 * D_M + D_A, 4 * D_M + 3 * D_A)


_CONV_PAD = 8


def _sample_attention_rows(q_ref, kvn_ref, cache_ref, bias_ref, cnt_ref, biasn_ref, cntn_ref, o_ref, b):
    t_new = q_ref.shape[1]
    scale = HD_A ** -0.5
    heads = [(h * HD_A, (h + 1) * HD_A) for h in range(H_A)]
    qh = [_bf(q_ref[b, :, lo:hi]) for lo, hi in heads]
    sc = jnp.concatenate([_dot(qh[h], _bf(cache_ref[0, b, 0, h])) for h in range(H_A)], axis=0)
    sn = jnp.concatenate([_dot_nt(qh[h], _bf(kvn_ref[b, :, lo:hi])) for h, (lo, hi) in enumerate(heads)], axis=0)
    sc = sc * scale + bias_ref[...]
    sn = sn * scale + biasn_ref[...]
    mx = jnp.maximum(jnp.max(sc, axis=1, keepdims=True), jnp.max(sn, axis=1, keepdims=True))
    pc = cnt_ref[...] * jnp.exp(sc - mx)
    pn = cntn_ref[...] * jnp.exp(sn - mx)
    den = jnp.sum(pc, axis=1, keepdims=True) + jnp.sum(pn, axis=1, keepdims=True)
    outs = []
    for h, (lo, hi) in enumerate(heads):
        r0, r1 = h * t_new, (h + 1) * t_new
        o = (_dot_nt(_bf(pc[r0:r1]), _bf(cache_ref[0, b, 1, h]))
             + _dot(_bf(pn[r0:r1]), _bf(kvn_ref[b, :, D_A + lo:D_A + hi])))
        outs.append(o / den[r0:r1])
    o_ref[b] = jnp.concatenate(outs, axis=1)


def _inproj_kernel(*refs, tm, guests):
    if not guests:
        x_ref, g_ref, w_ref, wg_ref, qk_ref, vm_ref, om_ref, qa_ref, kva_ref, gates_ref = refs
    else:
        (x_ref, g_ref, w_ref, wg_ref, sq_ref, skv_ref, cache_ref, bias_ref, cnt_ref, biasn_ref, cntn_ref,
         qk_ref, vm_ref, om_ref, qa_ref, kva_ref, gates_ref, sha_ref, zs) = refs
    a = _bf(_rms(x_ref[...], g_ref[...]))
    nq = D_A // LANES
    att_tile = lambda j: (qa_ref, j) if j < nq else (kva_ref, j - nq)
    z = _dot_nt(a, w_ref[_IN_CUTS[3]:_IN_CUTS[5], :])
    if not guests:
        for j in range(3 * nq):
            dst, jj = att_tile(j)
            dst[jj] = z[:, j * LANES:(j + 1) * LANES]
    else:
        for j in range(3 * nq):
            zs[j] = z[:, j * LANES:(j + 1) * LANES]
    qk_ref[...] = _dot_nt(a, w_ref[_IN_CUTS[0]:_IN_CUTS[1], :])
    if guests:
        sub = pl.program_id(0) % (ATT_TILE // tm)
        per = tm // ATT_STRIDE
        for j in range(3 * nq):
            dst, jj = att_tile(j)
            for r in range(ATT_STRIDE):
                start = pl.multiple_of(r * ATT_BLOCK + sub * per, SUBLANES)
                dst[jj, pl.ds(start, per), :] = zs[j, pl.ds(r, per, stride=ATT_STRIDE), :]
    vm_ref[...] = _dot_nt(a, w_ref[_IN_CUTS[1]:_IN_CUTS[2], :])
    for b in range(guests):
        _sample_attention_rows(sq_ref, skv_ref, cache_ref, bias_ref, cnt_ref, biasn_ref, cntn_ref, sha_ref, b)
    om_ref[...] = _dot_nt(a, w_ref[_IN_CUTS[2]:_IN_CUTS[3], :])
    gates_ref[...] = _dot_nt(a, wg_ref[...])


def _guest_plumbing(guest, steps):
    sq, skv, cache_t, layer, first, count = guest
    T = sq.shape[1]
    wb = cache_t.shape[-1]
    assert count % steps == 0 and first % (count // steps) == 0
    per_step = count // steps
    off = first // per_step
    bias, cnt = _sample_tables(wb, T)
    consts = [bias[:, :wb], cnt[:, :wb], bias[:, wb:], cnt[:, wb:]]
    per_g = lambda w: pl.BlockSpec((per_step, T, w), lambda i: (off + i, 0, 0))
    in_specs = [per_g(D_A), per_g(2 * D_A),
                pl.BlockSpec((1, per_step, 2, H_A, HD_A, wb), lambda i: (layer, off + i, 0, 0, 0, 0))]
    in_specs += [_resident(c.shape) for c in consts]
    args = [sq, skv, cache_t] + [jnp.asarray(c) for c in consts]
    out_spec = pl.BlockSpec((per_step, T, D_A), lambda i: (i, 0, 0))
    return per_step, in_specs, args, out_spec, jax.ShapeDtypeStruct((count, T, D_A), jnp.float32)


def _inproj(x, g, w_main, w_gates, tm, guest=None):
    n, d = x.shape
    row = lambda w: pl.BlockSpec((tm, w), lambda i: (i, 0))
    f32 = jnp.float32
    in_specs = [row(d), _resident((1, d)), _resident(w_main.shape), _resident(w_gates.shape)]
    out_shape = [jax.ShapeDtypeStruct((n, 2 * D_M), f32), jax.ShapeDtypeStruct((n, D_M), f32),
                 jax.ShapeDtypeStruct((n, D_M), f32), jax.ShapeDtypeStruct((D_A // LANES, n, LANES), f32),
                 jax.ShapeDtypeStruct((2 * D_A // LANES, n, LANES), f32), jax.ShapeDtypeStruct((n, GATE_PAD), f32)]
    args = [x, g, w_main, w_gates]
    scratch = []
    guests = 0
    if guest is None:
        tiles = lambda w: pl.BlockSpec((w // LANES, tm, LANES), lambda i: (0, i, 0))
        out_specs = [row(2 * D_M), row(D_M), row(D_M), tiles(D_A), tiles(2 * D_A), row(GATE_PAD)]
    else:
        assert n % ATT_TILE == 0 and ATT_TILE % tm == 0 and tm % (ATT_STRIDE * SUBLANES) == 0
        guests, g_specs, g_args, g_out_spec, g_out_shape = _guest_plumbing(guest, n // tm)
        per_att = ATT_TILE // tm
        tiles = lambda w: pl.BlockSpec((w // LANES, ATT_TILE, LANES), lambda i: (0, i // per_att, 0),
                                       pipeline_mode=pl.Buffered(1))
        in_specs += g_specs
        out_specs = [row(2 * D_M), row(D_M), row(D_M), tiles(D_A), tiles(2 * D_A), row(GATE_PAD), g_out_spec]
        out_shape.append(g_out_shape)
        args += g_args
        scratch = [pltpu.VMEM((3 * D_A // LANES, tm, LANES), f32)]
    return pl.pallas_call(
        functools.partial(_inproj_kernel, tm=tm, guests=guests),
        grid=(n // tm,),
        in_specs=in_specs,
        out_specs=out_specs,
        out_shape=out_shape,
        scratch_shapes=scratch,
        compiler_params=pltpu.CompilerParams(dimension_semantics=("arbitrary",), vmem_limit_bytes=VMEM_LIMIT),
        name="inproj",
    )(*args)


def _mlstm_head(q, k, v, a_col, b, a_row, m, cmat, nvec, causal, eye):
    L = q.shape[0]
    qb, kb, vb = _bf(q), _bf(k), _bf(v)
    if a_row is None:
        a_row = jnp.sum(jnp.where(eye, a_col, 0.0), axis=0, keepdims=True)
    dlog = jnp.where(causal, b + a_row, _NEG_INF)
    inter = b + m
    mt = jnp.maximum(inter, jnp.max(dlog, axis=1, keepdims=True))
    s = _dot_nt(qb, kb) * jnp.exp(dlog - mt)
    e_inter = jnp.exp(inter - mt)
    num = _dot(_bf(s), vb) + e_inter * _dot_nt(qb, _bf(cmat))
    den = jnp.sum(s, axis=1, keepdims=True) + e_inter * jnp.sum(q * nvec, axis=1, keepdims=True)
    hh = num / jnp.maximum(jnp.abs(den), jnp.exp(-mt))

    b_last = b[L - 1:L, :]
    wlog = b_last + a_col
    m_new = jnp.maximum(b_last + m, jnp.max(wlog, axis=0, keepdims=True))
    w = jnp.exp(wlog - m_new)
    decay = jnp.exp(b_last + m - m_new)
    wv = w * v
    if L < HD_M:
        pad = jnp.zeros((HD_M - L, HD_M), jnp.float32)
        wv_t = _bf(jnp.concatenate([wv, pad], axis=0).T)
        k_pad = _bf(jnp.concatenate([k, pad], axis=0))
    else:
        wv_t = _bf(wv.T)
        k_pad = kb
    c_new = decay * cmat + _dot(wv_t, k_pad)
    n_new = decay * nvec + jnp.sum(w * k, axis=0, keepdims=True)
    return hh, c_new, n_new, m_new


def _mlstm_kernel(qk_ref, v_ref, o_ref, g_ref, bias_ref, wconv_ref, gmh_ref, conv0_ref, c0_ref, n0_ref, m0_ref,
                  h_ref, c_out, n_out, m_out, conv_out, ubuf, c_s, n_s, m_s, *, chunk, group):
    L = chunk
    c = pl.program_id(1)
    hist = CONV_W - 1

    @pl.when(c == 0)
    def _():
        ubuf[:, _CONV_PAD - hist:_CONV_PAD, :] = conv0_ref[...]
        c_s[...] = c0_ref[0]
        n_s[...] = n0_ref[...]
        m_s[...] = m0_ref[...]

    row_id = lax.broadcasted_iota(jnp.int32, (L, GATE_PAD), 0)
    ti = lax.broadcasted_iota(jnp.int32, (L, L), 0)
    si = lax.broadcasted_iota(jnp.int32, (L, L), 1)
    causal = si <= ti
    eye = si == ti

    for g in range(group):
        ubuf[g, _CONV_PAD:_CONV_PAD + L, :] = qk_ref[g]
        conv = ubuf[g, _CONV_PAD - hist:_CONV_PAD - hist + L, :] * wconv_ref[0:1, :]
        for i in range(1, CONV_W):
            conv = conv + ubuf[g, _CONV_PAD - hist + i:_CONV_PAD - hist + i + L, :] * wconv_ref[i:i + 1, :]
        tail = ubuf[g, _CONV_PAD + L - hist:_CONV_PAD + L, :]
        ubuf[g, _CONV_PAD - hist:_CONV_PAD, :] = tail
        conv_out[g] = tail
        qk = conv * jax.nn.sigmoid(conv)

        gates = g_ref[g] + bias_ref[...]
        lf = jnp.minimum(gates, 0.0) - jnp.log(1.0 + jnp.exp(-jnp.abs(gates)))
        bcum = lf
        step = 1
        while step < L:
            bcum = bcum + jnp.where(row_id >= step, pltpu.roll(bcum, step, 0), 0.0)
            step *= 2

        a_all = gates - pltpu.roll(bcum, GATE_PAD - H_M, 1)
        a_rows = a_all.T if L % LANES == 0 else None

        for h in range(H_M):
            lanes = slice(h * HD_M, (h + 1) * HD_M)
            hh, c_new, n_new, m_new = _mlstm_head(
                qk[:, lanes], qk[:, D_M + h * HD_M:D_M + (h + 1) * HD_M] * (HD_M ** -0.5), v_ref[g, :, lanes],
                a_all[:, h:h + 1], bcum[:, H_M + h:H_M + h + 1], None if a_rows is None else a_rows[h:h + 1, :],
                m_s[g, h][:, 0:1], c_s[g, h], n_s[g, h], causal, eye)
            c_s[g, h] = c_new
            n_s[g, h] = n_new
            m_s[g, h] = jnp.broadcast_to(m_new, (1, LANES))
            h_ref[g, :, lanes] = jax.nn.sigmoid(o_ref[g, :, lanes]) * _rms(hh, gmh_ref[:, lanes])

    @pl.when(c == pl.num_programs(1) - 1)
    def _():
        c_out[...] = c_s[...]
        n_out[...] = n_s[...]
        m_out[...] = m_s[...]


def _mlstm_tiling(B, T):
    chunk = _row_tile(T, MLSTM_MAX_CHUNK)
    group = _row_tile(B, max(1, min(MLSTM_MAX_GROUP, MLSTM_GROUP_TOKENS // chunk)))
    return chunk, group


def _mlstm(qk, vm, om, gates, b_gates, w_conv, g_mh, conv0, c0, n0, m0, layer):
    B, T, _ = qk.shape
    chunk, G = _mlstm_tiling(B, T)
    tok = lambda w: pl.BlockSpec((G, chunk, w), lambda b, c: (b, c, 0))
    per_b = lambda shape: pl.BlockSpec((G,) + shape, lambda b, c: (b,) + (0,) * len(shape))
    hist = CONV_W - 1
    f32 = jnp.float32
    return pl.pallas_call(
        functools.partial(_mlstm_kernel, chunk=chunk, group=G),
        grid=(B // G, T // chunk),
        in_specs=[tok(2 * D_M), tok(D_M), tok(D_M), tok(GATE_PAD),
                  _resident((1, GATE_PAD)), _resident((CONV_W, 2 * D_M)), _resident((1, D_M)),
                  per_b((hist, 2 * D_M)),
                  pl.BlockSpec((1, G, H_M, HD_M, HD_M), lambda b, c: (layer, b, 0, 0, 0)),
                  per_b((H_M, 1, HD_M)), per_b((H_M, 1, LANES))],
        out_specs=[tok(D_M), per_b((H_M, HD_M, HD_M)), per_b((H_M, 1, HD_M)), per_b((H_M, 1, LANES)),
                   per_b((hist, 2 * D_M))],
        out_shape=[jax.ShapeDtypeStruct((B, T, D_M), f32),
                   jax.ShapeDtypeStruct((B, H_M, HD_M, HD_M), f32),
                   jax.ShapeDtypeStruct((B, H_M, 1, HD_M), f32),
                   jax.ShapeDtypeStruct((B, H_M, 1, LANES), f32),
                   jax.ShapeDtypeStruct((B, hist, 2 * D_M), f32)],
        scratch_shapes=[pltpu.VMEM((G, _CONV_PAD + chunk, 2 * D_M), f32),
                        pltpu.VMEM((G, H_M, HD_M, HD_M), f32),
                        pltpu.VMEM((G, H_M, 1, HD_M), f32),
                        pltpu.VMEM((G, H_M, 1, LANES), f32)],
        compiler_params=pltpu.CompilerParams(dimension_semantics=("arbitrary", "arbitrary"),
                                             vmem_limit_bytes=VMEM_LIMIT),
        name="mlstm",
    )(qk, vm, om, gates, b_gates, w_conv, g_mh, conv0, c0, n0, m0)


def _mlstm_short_kernel(qk_ref, v_ref, o_ref, g_ref, hist_ref, bias_ref, wconv_ref, gmh_ref, c0_ref, n0_ref, m0_ref,
                        c_alias_ref, h_ref, c_out, n_out, m_out, qx, kx, *, group):
    G, T = group, SUBLANES
    R = G * T
    f32 = jnp.float32

    @pl.when(pl.program_id(0) == 0)
    def _():
        qx[...] = jnp.zeros(qx.shape, f32)
        kx[...] = jnp.zeros(kx.shape, f32)

    grp = lambda x: x.reshape(G, T, x.shape[-1])
    rows_of = lambda x: jnp.broadcast_to(x[:, None, :], (G, T, x.shape[-1])).reshape(R, x.shape[-1])
    shift_gate = lambda x: pltpu.roll(x, GATE_PAD - H_M, 1)

    t_wide = lax.broadcasted_iota(jnp.int32, (R, 2 * D_M), 0) % T
    u = qk_ref[...]
    hist = hist_ref[...]
    conv = u * wconv_ref[CONV_W - 1:CONV_W, :]
    for k in range(1, CONV_W):
        back = jnp.where(t_wide >= k, pltpu.roll(u, k, 0), pltpu.roll(hist, (R - (CONV_W - 1 - k)) % R, 0))
        conv = conv + back * wconv_ref[CONV_W - 1 - k:CONV_W - k, :]
    qk = conv * jax.nn.sigmoid(conv)

    gates = g_ref[...] + bias_ref[...]
    lf = jnp.minimum(gates, 0.0) - jnp.log(1.0 + jnp.exp(-jnp.abs(gates)))
    t_id = lax.broadcasted_iota(jnp.int32, (R, GATE_PAD), 0) % T
    bcum = lf
    step = 1
    while step < T:
        bcum = bcum + jnp.where(t_id >= step, pltpu.roll(bcum, step, 0), 0.0)
        step *= 2

    a_all = gates - shift_gate(bcum)
    a_rows = a_all.T
    b_last = shift_gate(rows_of(grp(bcum)[:, T - 1, :]))
    m_rows = rows_of(m0_ref[...])
    wlog = b_last + a_all
    m_new = jnp.maximum(b_last + m_rows, rows_of(jnp.max(grp(wlog), axis=1)))
    w_all = jnp.exp(wlog - m_new)
    decay = grp(jnp.exp(b_last + m_rows - m_new))[:, 0, :]
    m_out[...] = grp(m_new)[:, 0, :]

    ri = lax.broadcasted_iota(jnp.int32, (R, R), 0)
    ci = lax.broadcasted_iota(jnp.int32, (R, R), 1)
    mask = (ri // T == ci // T) & (ci <= ri)

    for h in range(H_M):
        lanes = slice(h * HD_M, (h + 1) * HD_M)
        q = qk[:, lanes]
        k = qk[:, D_M + h * HD_M:D_M + (h + 1) * HD_M] * (HD_M ** -0.5)
        v = v_ref[:, lanes]
        qb, kb, vb = _bf(q), _bf(k), _bf(v)
        b = bcum[:, H_M + h:H_M + h + 1]
        m = m_rows[:, h:h + 1]
        dlog = jnp.where(mask, b + a_rows[h:h + 1, :], _NEG_INF)
        inter = b + m
        mt = jnp.maximum(inter, jnp.max(dlog, axis=1, keepdims=True))
        s = _dot_nt(qb, kb) * jnp.exp(dlog - mt)
        e_inter = jnp.exp(inter - mt)
        for g in range(G):
            qx[g * T:(g + 1) * T, g * HD_M:(g + 1) * HD_M] = q[g * T:(g + 1) * T, :]
            kx[g * T:(g + 1) * T, g * HD_M:(g + 1) * HD_M] = k[g * T:(g + 1) * T, :]
        c_cat = jnp.concatenate([c0_ref[0, g, h] for g in range(G)], axis=1)
        n_h = n0_ref[h]
        num = _dot(_bf(s), vb) + e_inter * _dot_nt(_bf(qx[...]), _bf(c_cat))
        den = jnp.sum(s, axis=1, keepdims=True) + e_inter * jnp.sum(q * rows_of(n_h), axis=1, keepdims=True)
        hh = num / jnp.maximum(jnp.abs(den), jnp.exp(-mt))
        h_ref[:, lanes] = jax.nn.sigmoid(o_ref[:, lanes]) * _rms(hh, gmh_ref[:, lanes])

        w = w_all[:, h:h + 1]
        upd = _dot(_bf((w * v).T), _bf(kx[...]))
        for g in range(G):
            c_out[0, g, h] = decay[g:g + 1, h:h + 1] * c0_ref[0, g, h] + upd[:, g * HD_M:(g + 1) * HD_M]
        n_out[h] = decay[:, h:h + 1] * n_h + jnp.sum(grp(w * k), axis=1)


def _mlstm_short(qk, vm, om, gates, b_gates, w_conv, g_mh, hist, c0, n0, m0, layer, B, c_all_new):
    T = SUBLANES
    G = LANES // T
    assert B % G == 0 and qk.shape[0] == B * T
    R = G * T
    row = lambda w: pl.BlockSpec((R, w), lambda i: (i, 0))
    f32 = jnp.float32
    in_specs = [row(2 * D_M), row(D_M), row(D_M), row(GATE_PAD), row(2 * D_M),
                _resident((1, GATE_PAD)), _resident((CONV_W, 2 * D_M)), _resident((1, D_M)),
                pl.BlockSpec((1, G, H_M, HD_M, HD_M), lambda i: (layer, i, 0, 0, 0)),
                pl.BlockSpec((H_M, G, HD_M), lambda i: (0, i, 0)),
                pl.BlockSpec((G, GATE_PAD), lambda i: (i, 0))]
    args = [qk, vm, om, gates, hist, b_gates, w_conv, g_mh, c0, n0, m0, c_all_new]
    aliases = {len(args) - 1: 1}
    in_specs.append(pl.BlockSpec(memory_space=pl.ANY))
    return pl.pallas_call(
        functools.partial(_mlstm_short_kernel, group=G),
        grid=(B // G,),
        in_specs=in_specs,
        out_specs=[row(D_M), pl.BlockSpec((1, G, H_M, HD_M, HD_M), lambda i: (layer, i, 0, 0, 0)),
                   pl.BlockSpec((H_M, G, HD_M), lambda i: (0, i, 0)), pl.BlockSpec((G, GATE_PAD), lambda i: (i, 0))],
        out_shape=[jax.ShapeDtypeStruct((B * T, D_M), f32), jax.ShapeDtypeStruct(c0.shape, f32),
                   jax.ShapeDtypeStruct((H_M, B, HD_M), f32), jax.ShapeDtypeStruct((B, GATE_PAD), f32)],
        input_output_aliases=aliases,
        scratch_shapes=[pltpu.VMEM((R, G * HD_M), f32), pltpu.VMEM((R, G * HD_M), f32)],
        compiler_params=pltpu.CompilerParams(dimension_semantics=("arbitrary",), vmem_limit_bytes=VMEM_LIMIT),
        name="mlstm_short",
    )(*args)


def _pattn_kernel(q_ref, kv_ref, kvt_alias_ref, o_ref, kvt_ref, prev_s, acc_s, m_s, l_s, s_buf, p_buf, m_buf, tok_s):
    i = pl.program_id(1)
    nb = ATT_BLOCK
    pair = 2 * HD_A
    scale = HD_A ** -0.5

    @pl.when(i == 0)
    def _():
        prev_s[...] = jnp.zeros(prev_s.shape, prev_s.dtype)

    pq = lax.broadcasted_iota(jnp.int32, (nb, nb), 0)
    pk = lax.broadcasted_iota(jnp.int32, (nb, nb), 1)
    lo_half = lax.broadcasted_iota(jnp.int32, (nb, pair), 1) < HD_A
    ones = jnp.ones((2 * nb, pair), jnp.bfloat16)
    no_keys = jnp.full((nb, nb), _NEG_INF, jnp.float32)
    npat = len(DILATED_PATTERNS)
    npair = H_A // 2

    for pi, (win, dil) in enumerate(DILATED_PATTERNS):
        first, last = pi == 0, pi == npat - 1
        nrun = ATT_STRIDE // dil
        rlen = nb // nrun
        nblk = nrun
        dist = nrun * (pq % rlen) + pq // rlen - (nrun * (pk % rlen) + pk // rlen)
        nd_cur = jnp.where(dist >= 0, -(dist * dil).astype(jnp.float32), _NEG_INF)
        nd_prev = jnp.where(dist <= 0, -((dist + nb) * dil).astype(jnp.float32), _NEG_INF)
        nd_prev_first = jnp.where(i > 0, nd_prev, no_keys)

        def starts(c, jb, dil=dil, nrun=nrun, rlen=rlen):
            rows = [(dil * e + c) * nb + rlen * jb for e in range(nrun)]
            return [r if isinstance(r, int) else pl.multiple_of(r, SUBLANES) for r in rows]

        def load(ref, j, st, rlen=rlen):
            return jnp.concatenate([ref[j, pl.ds(s, rlen), :] for s in st], axis=0)

        def store(ref, j, st, val, rlen=rlen):
            for e, s in enumerate(st):
                ref[j, pl.ds(s, rlen), :] = val[e * rlen:(e + 1) * rlen]

        def unit(c, jb, prev_ref, prev_jb, nd_prev_u, slot, first=first, last=last, nd_cur=nd_cur, starts=starts,
                 load=load, store=store):
            sb = slot * H_A
            rows = starts(c, jb)
            prows = starts(c, prev_jb)
            nd = jnp.concatenate([nd_cur, nd_prev_u], axis=1)
            for hp in range(npair):
                q2 = load(q_ref, hp, rows) * scale
                k2 = _bf(jnp.concatenate([load(kv_ref, hp, rows), load(prev_ref, hp, prows)], axis=0))
                for e in range(2):
                    qm = _bf(jnp.where(lo_half if e == 0 else jnp.logical_not(lo_half), q2, 0.0))
                    s_buf[sb + 2 * hp + e] = _dot_nt(qm, k2) + _SLOPES[2 * hp + e] * nd
            for h in range(H_A):
                s = s_buf[sb + h]
                m = jnp.max(s, axis=1, keepdims=True)
                p_buf[sb + h] = _bf(jnp.exp(s - m))
                m_buf[sb + h] = jnp.broadcast_to(m, (nb, pair))
            for hp in range(npair):
                v2 = _bf(jnp.concatenate([load(kv_ref, npair + hp, rows), load(prev_ref, npair + hp, prows)], axis=0))
                rhs = jnp.concatenate([v2, ones], axis=1)
                r_e = _dot(p_buf[sb + 2 * hp], rhs)
                r_o = _dot(p_buf[sb + 2 * hp + 1], rhs)
                acc2 = jnp.where(lo_half, r_e[:, :pair], r_o[:, :pair])
                l2 = jnp.where(lo_half, r_e[:, pair:], r_o[:, pair:])
                m2 = jnp.where(lo_half, m_buf[sb + 2 * hp], m_buf[sb + 2 * hp + 1])
                if not first:
                    m_old = load(m_s, hp, rows)
                    m_new = jnp.maximum(m_old, m2)
                    a_old = jnp.exp(m_old - m_new)
                    a_loc = jnp.exp(m2 - m_new)
                    acc2 = a_old * load(acc_s, hp, rows) + a_loc * acc2
                    l2 = a_old * load(l_s, hp, rows) + a_loc * l2
                    m2 = m_new
                if last:
                    o_ref[hp, pl.ds(c, nb, stride=ATT_STRIDE), :] = acc2 / l2
                else:
                    store(acc_s, hp, rows, acc2)
                    store(m_s, hp, rows, m2)
                    store(l_s, hp, rows, l2)

        def first_block(c, slot, nblk=nblk, unit=unit, nd_prev_first=nd_prev_first):
            unit(c, 0, prev_s, nblk - 1, nd_prev_first, slot)

        def later_block(u, slot, dil=dil, unit=unit, nd_prev=nd_prev):
            jb = 1 + u // dil
            unit(u % dil, jb, kv_ref, jb - 1, nd_prev, slot)

        def run(count, block):
            def body(p, carry):
                for slot in range(ATT_UNROLL):
                    block(ATT_UNROLL * p + slot, slot)
                return carry
            if count // ATT_UNROLL:
                lax.fori_loop(0, count // ATT_UNROLL, body, 0)
            for slot in range(count % ATT_UNROLL):
                block(count - count % ATT_UNROLL + slot, slot)

        run(dil, first_block)
        if nblk > 1:
            run(dil * (nblk - 1), later_block)

    prev_s[...] = kv_ref[...]

    @pl.when(i == pl.num_programs(1) - 1)
    def _():
        for j in range(2 * npair):
            for r in range(ATT_STRIDE):
                tok_s[pl.ds(r, nb, stride=ATT_STRIDE), :] = kv_ref[j, r * nb:(r + 1) * nb, :]
            t = tok_s[...].T
            kvt_ref[0, 0, j // npair, 2 * (j % npair)] = t[0:HD_A]
            kvt_ref[0, 0, j // npair, 2 * (j % npair) + 1] = t[HD_A:pair]


def _prompt_attention(q, kv, B, layer, depth, kvt_all):
    nq, n, _ = q.shape
    S = n // B
    for win, dil in DILATED_PATTERNS:
        assert win // dil == ATT_BLOCK and ATT_TILE % (dil * ATT_BLOCK) == 0
    assert S % ATT_TILE == 0 and LANES == 2 * HD_A
    assert DILATED_PATTERNS[-1][1] == ATT_STRIDE and all(ATT_STRIDE % d == 0 for _, d in DILATED_PATTERNS)
    nt = S // ATT_TILE
    tile = lambda k: pl.BlockSpec((k, ATT_TILE, LANES), lambda b, i: (0, b * nt + i, 0))
    f32 = jnp.float32
    in_specs = [tile(nq), tile(2 * nq), pl.BlockSpec(memory_space=pl.ANY)]
    args = [q, kv, kvt_all]
    aliases = {2: 1}
    return pl.pallas_call(
        _pattn_kernel,
        grid=(B, nt),
        in_specs=in_specs,
        out_specs=[tile(nq), pl.BlockSpec((1, 1, 2, H_A, HD_A, ATT_TILE), lambda b, i: (layer, b, 0, 0, 0, 0),
                                          pipeline_mode=pl.Buffered(1))],
        out_shape=[jax.ShapeDtypeStruct((nq, n, LANES), f32),
                   jax.ShapeDtypeStruct((depth, B, 2, H_A, HD_A, ATT_TILE), f32)],
        input_output_aliases=aliases,
        scratch_shapes=[pltpu.VMEM((2 * nq, ATT_TILE, LANES), f32), pltpu.VMEM((nq, ATT_TILE, LANES), f32),
                        pltpu.VMEM((nq, ATT_TILE, LANES), f32), pltpu.VMEM((nq, ATT_TILE, LANES), f32),
                        pltpu.VMEM((ATT_UNROLL * H_A, ATT_BLOCK, 2 * ATT_BLOCK), f32),
                        pltpu.VMEM((ATT_UNROLL * H_A, ATT_BLOCK, 2 * ATT_BLOCK), jnp.bfloat16),
                        pltpu.VMEM((ATT_UNROLL * H_A, ATT_BLOCK, LANES), f32), pltpu.VMEM((ATT_TILE, LANES), f32)],
        compiler_params=pltpu.CompilerParams(dimension_semantics=("arbitrary", "arbitrary"),
                                             vmem_limit_bytes=VMEM_LIMIT),
        name="pattn",
    )(*args)


def _sample_tables(wb, t_new):
    rows = np.arange(H_A * t_new)
    t = rows % t_new
    slope = np.asarray(_SLOPES)[rows // t_new]
    pos = np.arange(wb + t_new)
    d = wb + t[:, None] - pos[None, :]
    cnt = np.zeros(d.shape, np.float32)
    for win, dil in DILATED_PATTERNS:
        cnt += ((d >= 0) & (d % dil == 0) & (d <= win)).astype(np.float32)
    bias = np.where(cnt > 0, -slope[:, None] * d, -np.inf).astype(np.float32)
    return bias, cnt


def _outffn_kernel(x_ref, hm_ref, ha_ref, wo_ref, g1_ref, g2_ref, g3_ref, wg_ref, wu_ref, wd_ref, y_ref):
    ha = jnp.concatenate([ha_ref[j] for j in range(D_A // LANES)], axis=1)
    mix = _dot(_bf(hm_ref[...]), wo_ref[0:D_M, :]) + _dot(_bf(ha), wo_ref[D_M:D_M + D_A, :])
    x1 = x_ref[...] + _rms(mix, g1_ref[...])
    hf = _bf(_rms(x1, g2_ref[...]))
    gate = _dot(hf, wg_ref[...])
    up = _dot(hf, wu_ref[...])
    f = _dot(_bf(gate * jax.nn.sigmoid(gate) * up), wd_ref[...])
    y_ref[...] = x1 + _rms(f, g3_ref[...])


def _outffn(x, hm, ha, w_out, g_mix_post, g_ffn_pre, g_ffn_post, w_gate, w_up, w_down, tm):
    n, d = x.shape
    row = lambda w: pl.BlockSpec((tm, w), lambda i: (i, 0))
    return pl.pallas_call(
        _outffn_kernel,
        grid=(n // tm,),
        in_specs=[row(d), row(D_M), pl.BlockSpec((D_A // LANES, tm, LANES), lambda i: (0, i, 0)),
                  _resident(w_out.shape), _resident((1, d)), _resident((1, d)),
                  _resident((1, d)), _resident(w_gate.shape), _resident(w_up.shape), _resident(w_down.shape)],
        out_specs=row(d),
        out_shape=jax.ShapeDtypeStruct((n, d), jnp.float32),
        compiler_params=pltpu.CompilerParams(dimension_semantics=("arbitrary",), vmem_limit_bytes=VMEM_LIMIT),
        name="outffn",
    )(x, hm, ha, w_out, g_mix_post, g_ffn_pre, g_ffn_post, w_gate, w_up, w_down)


def _layer(xp, xs, lw, conv_s, c_all, n_s, m_s, cache, layer, kvt_all, c_all_new):
    B, S, D = xp.shape
    BS, T, _ = xs.shape
    assert T == SUBLANES and BS % (LANES // SUBLANES) == 0 and MAX_WINDOW == ATT_TILE
    f32 = jnp.float32
    hist = CONV_W - 1
    n_p, n_sm = B * S, BS * T
    xpf, xsf = xp.reshape(n_p, D), xs.reshape(n_sm, D)
    rows = lambda a, b, t: jnp.transpose(a, (1, 0, 2)).reshape(b, t, a.shape[0] * LANES)
    ffn = lambda x, hm, ha, tm: _outffn(x, hm, ha, lw["w_out"], lw["g_mix_post"], lw["g_ffn_pre"], lw["g_ffn_post"],
                                        lw["w_gate"], lw["w_up"], lw["w_down"], tm)
    qk_s, vm_s, om_s, qa_s, kva_s, gates_s = _inproj(xsf, lw["g_mix_pre"], lw["w_main"], lw["w_gates"],
                                                     _row_tile(n_sm, 1024))
    kv_rows = rows(kva_s, BS, T)
    qk, vm, om, qa, kva, gates, ha_s = _inproj(xpf, lw["g_mix_pre"], lw["w_main"], lw["w_gates"],
                                               _row_tile(S, HOST_ROWS),
                                               guest=(rows(qa_s, BS, T), kv_rows, cache, layer, 0, BS))

    r3 = lambda a: a.reshape(B, S, a.shape[-1])
    hm, c_p, n_p_new, m_p, conv_p = _mlstm(
        r3(qk), r3(vm), r3(om), r3(gates), lw["b_gates"], lw["w_conv"], lw["g_mh"],
        jnp.zeros((B, hist, 2 * D_M), f32), jnp.zeros((1, B, H_M, HD_M, HD_M), f32),
        jnp.zeros((B, H_M, 1, HD_M), f32), jnp.zeros((B, H_M, 1, LANES), f32), 0)
    ha, kvt_all = _prompt_attention(qa, kva, B, layer, c_all.shape[0], kvt_all)
    yp = ffn(xpf, hm.reshape(n_p, D_M), ha, _row_tile(n_p, 512))
    out_p = (kvt_all, c_p, n_p_new.reshape(B, H_M, HD_M), m_p[:, :, 0, 0], conv_p)

    conv_rows = jnp.pad(conv_s, ((0, 0), (0, T - hist), (0, 0))).reshape(n_sm, 2 * D_M)
    hm_s, c_s, n_s_new, m_s_new = _mlstm_short(
        qk_s, vm_s, om_s, gates_s, lw["b_gates"], lw["w_conv"], lw["g_mh"], conv_rows, c_all,
        jnp.transpose(n_s, (1, 0, 2)), jnp.pad(m_s, ((0, 0), (0, GATE_PAD - H_M))), layer, BS, c_all_new)
    ys = ffn(xsf, hm_s, jnp.transpose(ha_s.reshape(n_sm, D_A // LANES, LANES), (1, 0, 2)), _row_tile(n_sm, 512))
    out_s = (kv_rows.reshape(BS, T, 2, H_A, HD_A), c_s, jnp.transpose(n_s_new, (1, 0, 2)), m_s_new[:, :H_M],
             qk_s.reshape(BS, T, 2 * D_M)[:, T - hist:])
    return yp.reshape(B, S, D), ys.reshape(BS, T, D), out_p, out_s


def kernel(x_prompt, x_sample, cache_kv, state_C, state_n, state_m, state_conv, g_mix_pre, g_mix_post, g_ffn_pre,
           g_ffn_post, w_in, b_gates, w_conv, g_mh, w_out, w_gate, w_up, w_down):
    depth = w_in.shape[0]
    g0 = 4 * D_M
    g1 = g0 + 2 * H_M
    cache = jnp.transpose(cache_kv, (0, 1, 3, 4, 5, 2))
    w_in_t = jnp.transpose(w_in, (0, 2, 1))
    w_main_all = _bf(jnp.concatenate([w_in_t[:, :g0], w_in_t[:, g1:]], axis=1))
    w_gates_all = _bf(jnp.pad(w_in_t[:, g0:g1], ((0, 0), (0, GATE_PAD - 2 * H_M), (0, 0))))

    hp, hs = x_prompt, x_sample
    outs_p, outs_s = [], []
    kvt_all = jnp.zeros((depth, x_prompt.shape[0], 2, H_A, HD_A, ATT_TILE), jnp.float32)
    c_all_new = jnp.zeros(state_C.shape, jnp.float32)
    for l in range(depth):
        lw = {
            "g_mix_pre": g_mix_pre[l][None], "g_mix_post": g_mix_post[l][None],
            "g_ffn_pre": g_ffn_pre[l][None], "g_ffn_post": g_ffn_post[l][None],
            "w_main": w_main_all[l], "w_gates": w_gates_all[l],
            "b_gates": jnp.pad(b_gates[l], (0, GATE_PAD - 2 * H_M))[None],
            "w_conv": w_conv[l], "g_mh": g_mh[l][None],
            "w_out": _bf(w_out[l]), "w_gate": _bf(w_gate[l]), "w_up": _bf(w_up[l]), "w_down": _bf(w_down[l]),
        }
        hp, hs, out_p, out_s = _layer(hp, hs, lw, state_conv[l], state_C, state_n[l], state_m[l], cache, l,
                                      kvt_all, c_all_new)
        kvt_all, c_all_new = out_p[0], out_s[1]
        outs_p.append(out_p)
        outs_s.append(out_s)
    stack = lambda outs, i: jnp.stack([o[i] for o in outs])
    kv_prompt = jnp.transpose(kvt_all, (0, 1, 5, 2, 3, 4))
    return (hp, hs, kv_prompt, stack(outs_s, 0),
            stack(outs_p, 1), stack(outs_p, 2), stack(outs_p, 3), stack(outs_p, 4),
            c_all_new, stack(outs_s, 2), stack(outs_s, 3), stack(outs_s, 4))
```
